```python
import functools
import jax
import jax.numpy as jnp
from jax import lax
import numpy as np

D_MODEL = 1024
BATCH = 1
SEQ = 16384
DEPTH = 1
DEC_BATCH = 128
DEC_SEQ = 1
PAST_LEN = 16384
PAGE_SIZE = 128

N_HEADS = 16
HEAD_DIM = 64
N_KV_HEADS = 4
Q_PER_KV = N_HEADS // N_KV_HEADS
ATTN_WIDTH = N_HEADS * HEAD_DIM
KV_WIDTH = N_KV_HEADS * HEAD_DIM
WINDOW = 128
ATTN_BLOCK = WINDOW
ROPE_THETA = 500000.0
ROT_DIM = HEAD_DIM // 4
POOL_WINDOWS = (2, 4, 8, 16)
N_POOL_GROUPS = len(POOL_WINDOWS)
POOL_WIDTH = D_MODEL
POOL_GROUP_DIM = POOL_WIDTH // N_POOL_GROUPS
POOL_HALO = max(POOL_WINDOWS) - 1
N_BRANCHES = 2
SPLIT_POINTS = (ATTN_WIDTH,
                ATTN_WIDTH + KV_WIDTH,
                ATTN_WIDTH + 2 * KV_WIDTH,
                ATTN_WIDTH + 2 * KV_WIDTH + POOL_WIDTH)
IN_WIDTH = ATTN_WIDTH + 2 * KV_WIDTH + POOL_WIDTH + N_BRANCHES * D_MODEL
N_EXPERTS = 256
TOP_K = 8
N_EXPERT_GROUPS = 8
TOPK_GROUPS = 4
EXPERT_DIM = 256
SHARED_DIM = 256
ROUTED_SCALE = 2.5
EXPERT_BLOCK = 128
LN_EPS = 1e-5
DEEPNORM_ALPHA = (2.0 * DEPTH) ** 0.25
DEEPNORM_BETA = (8.0 * DEPTH) ** -0.25

kernel_name = "hybrid_swa_sink_pool_moe_step"


def layer_norm(x, g, b):
    xf = x.astype(jnp.float32)
    xc = xf - xf.mean(-1, keepdims=True)
    var = (xc * xc).mean(-1, keepdims=True)
    return (xc * lax.rsqrt(var + LN_EPS) * g.astype(jnp.float32) + b.astype(jnp.float32)).astype(x.dtype)


def rope(x, pos):
    half = ROT_DIM // 2
    inv = ROPE_THETA ** (-jnp.arange(0, ROT_DIM, 2, dtype=jnp.float32) / ROT_DIM)
    ang = pos.astype(jnp.float32)[:, None] * inv[None, :]
    cos = jnp.cos(ang)[:, None, :]
    sin = jnp.sin(ang)[:, None, :]
    xr = x[..., :ROT_DIM].astype(jnp.float32)
    x1, x2 = xr[..., :half], xr[..., half:]
    rot = jnp.concatenate([x1 * cos - x2 * sin, x2 * cos + x1 * sin], axis=-1).astype(x.dtype)
    return jnp.concatenate([rot, x[..., ROT_DIM:]], axis=-1)


def sink_attention(q, k, v, mask, sinks):
    s = jnp.einsum('...qkgd,...jkd->...kgqj', q, k, preferred_element_type=jnp.float32) * (HEAD_DIM ** -0.5)
    s = jnp.where(mask, s, -jnp.inf)
    sink = sinks.astype(jnp.float32).reshape(N_KV_HEADS, Q_PER_KV, 1, 1)
    m = jnp.maximum(s.max(-1, keepdims=True), sink)
    p = jnp.exp(s - m)
    p = (p / (p.sum(-1, keepdims=True) + jnp.exp(sink - m))).astype(v.dtype)
    o = jnp.einsum('...kgqj,...jkd->...qkgd', p, v)
    return o.reshape(o.shape[:-3] + (ATTN_WIDTH,))


def swa_prompt(q, k, v, sinks):
    B, S = k.shape[:2]
    nb = S // ATTN_BLOCK
    qb = q.reshape(B, nb, ATTN_BLOCK, N_KV_HEADS, Q_PER_KV, HEAD_DIM)
    kb = k.reshape(B, nb, ATTN_BLOCK, N_KV_HEADS, HEAD_DIM)
    vb = v.reshape(B, nb, ATTN_BLOCK, N_KV_HEADS, HEAD_DIM)
    pad = ((0, 0), (1, 0), (0, 0), (0, 0), (0, 0))
    k_band = jnp.concatenate([jnp.pad(kb[:, :-1], pad), kb], axis=2)
    v_band = jnp.concatenate([jnp.pad(vb[:, :-1], pad), vb], axis=2)
    a = jnp.arange(ATTN_BLOCK)[:, None]
    j = jnp.arange(2 * ATTN_BLOCK)[None, :]
    rel = ATTN_BLOCK + a - j
    band = (rel >= 0) & (rel <= WINDOW)
    exists = (jnp.arange(nb)[:, None, None] > 0) | (j[None] >= ATTN_BLOCK)
    mask = (band[None] & exists)[None, :, None, None]
    o = sink_attention(qb, k_band, v_band, mask, sinks)
    return o.reshape(B, S, ATTN_WIDTH)


def multi_scale_pool(ext, pos, w_pool, pool_scale):
    B = ext.shape[0]
    L = ext.shape[1] - POOL_HALO
    cs = jnp.pad(jnp.cumsum(ext.astype(jnp.float32), axis=1), ((0, 0), (1, 0), (0, 0)))
    end = cs[:, POOL_HALO + 1:]
    u = ext[:, POOL_HALO:].astype(jnp.float32)
    outs = []
    for g, w in enumerate(POOL_WINDOWS):
        sl = slice(g * POOL_GROUP_DIM, (g + 1) * POOL_GROUP_DIM)
        start = cs[:, POOL_HALO + 1 - w:POOL_HALO + 1 - w + L, sl]
        cnt = jnp.minimum(w, pos + 1).astype(jnp.float32)[None, :, None]
        outs.append((end[..., sl] - start) / cnt - u[..., sl])
    d = jnp.stack(outs, axis=2).astype(ext.dtype)
    y = jnp.einsum('blgc,gce->blge', d, w_pool).reshape(B, L, POOL_WIDTH)
    return y * pool_scale


def prompt_mixer(q, k, v, u, sinks, w_pool, pool_scale):
    B, S = u.shape[:2]
    attn = swa_prompt(q, k, v, sinks)
    ext = jnp.pad(u, ((0, 0), (POOL_HALO, 0), (0, 0)))
    pool = multi_scale_pool(ext, jnp.arange(S), w_pool, pool_scale)
    wb = min(WINDOW, S)
    return attn, pool, (k[:, S - wb:], v[:, S - wb:], u[:, S - POOL_HALO:])


def sample_mixer(q, k, v, u, cache_k, cache_v, state_pool, sinks, w_pool, pool_scale):
    L = u.shape[1]
    wb = cache_k.shape[1]
    k_all = jnp.concatenate([cache_k, k], axis=1)
    v_all = jnp.concatenate([cache_v, v], axis=1)
    qpos = PAST_LEN + jnp.arange(L)
    kpos = jnp.concatenate([PAST_LEN - wb + jnp.arange(wb), qpos])
    rel = qpos[:, None] - kpos[None, :]
    mask = ((rel >= 0) & (rel <= WINDOW))[None, None, None]
    attn = sink_attention(q, k_all, v_all, mask, sinks)
    ext = jnp.concatenate([state_pool, u], axis=1)
    pool = multi_scale_pool(ext, qpos, w_pool, pool_scale)
    return attn, pool, (k_all[:, L:], v_all[:, L:], ext[:, L:])


def project(h, w_in, pos):
    B, L = h.shape[:2]
    z = h @ w_in
    q, k, v, u, g = jnp.split(z, SPLIT_POINTS, axis=-1)
    q = rope(q.reshape(B, L, N_HEADS, HEAD_DIM), pos).reshape(B, L, N_KV_HEADS, Q_PER_KV, HEAD_DIM)
    k = rope(k.reshape(B, L, N_KV_HEADS, HEAD_DIM), pos)
    v = v.reshape(B, L, N_KV_HEADS, HEAD_DIM)
    return q, k, v, u, g


def swiglu(x, wg, wu, wd):
    return (jax.nn.silu(x @ wg) * (x @ wu)) @ wd


def route(h2d, w_router, router_bias):
    T = h2d.shape[0]
    scores = jax.nn.sigmoid((h2d @ w_router).astype(jnp.float32))
    sel = scores + router_bias.astype(jnp.float32)
    gscore = lax.top_k(sel.reshape(T, N_EXPERT_GROUPS, -1), 2)[0].sum(-1)
    _, gidx = lax.top_k(gscore, TOPK_GROUPS)
    gmask = jax.nn.one_hot(gidx, N_EXPERT_GROUPS, dtype=jnp.float32).sum(1) > 0
    emask = jnp.repeat(gmask, N_EXPERTS // N_EXPERT_GROUPS, axis=1)
    _, idx = lax.top_k(jnp.where(emask, sel, -jnp.inf), TOP_K)
    w = jnp.take_along_axis(scores, idx, axis=1)
    w = w / w.sum(-1, keepdims=True) * ROUTED_SCALE
    return idx, w


def routed_experts(h2d, idx, wts, w_gate, w_up, w_down):
    T, D = h2d.shape
    TK = T * TOP_K
    flat_e = idx.reshape(-1)
    flat_tok = jnp.arange(TK) // TOP_K
    flat_w = wts.reshape(-1)
    order = jnp.argsort(flat_e, stable=True)
    se, stok, sw = flat_e[order], flat_tok[order], flat_w[order]
    counts = jnp.bincount(flat_e, length=N_EXPERTS)
    starts = jnp.cumsum(counts) - counts
    padded = (counts + EXPERT_BLOCK - 1) // EXPERT_BLOCK * EXPERT_BLOCK
    pends = jnp.cumsum(padded)
    pstarts = pends - padded
    dest = pstarts[se] + jnp.arange(TK) - starts[se]
    nb = -(-TK // EXPERT_BLOCK) + N_EXPERTS
    row_tok = jnp.full((nb * EXPERT_BLOCK,), T, jnp.int32).at[dest].set(stok.astype(jnp.int32))
    row_w = jnp.zeros((nb * EXPERT_BLOCK,), jnp.float32).at[dest].set(sw)
    block_e = jnp.minimum(jnp.searchsorted(pends, jnp.arange(nb) * EXPERT_BLOCK, side='right'), N_EXPERTS - 1)

    def one_block(args):
        tok, wt, e = args
        xb = h2d[jnp.minimum(tok, T - 1)]
        yb = swiglu(xb, w_gate[e], w_up[e], w_down[e])
        return yb * wt[:, None].astype(yb.dtype)

    y = lax.map(one_block, (row_tok.reshape(nb, EXPERT_BLOCK), row_w.reshape(nb, EXPERT_BLOCK), block_e))
    return jax.ops.segment_sum(y.reshape(-1, D), row_tok, num_segments=T + 1)[:T]


def decoder_layer(x, c, pos, mixer_core, w_ada, b_ada, w_in, w_out, ln1_g, ln1_b,
                  w_router, router_bias, w_exp_gate, w_exp_up, w_exp_down,
                  w_sh_gate, w_sh_up, w_sh_down, ln2_g, ln2_b):
    mods = jax.nn.silu(c) @ w_ada + b_ada
    shift1, scale1, gate1, shift2, scale2, gate2 = [m[:, None, :] for m in jnp.split(mods, 6, axis=-1)]
    h = x * (1 + scale1) + shift1
    q, k, v, u, g = project(h, w_in, pos)
    attn, pool, states = mixer_core(q, k, v, u)
    ga, gb = jnp.split(g, N_BRANCHES, axis=-1)
    y = (jax.nn.sigmoid(ga) * attn + jax.nn.sigmoid(gb) * pool) @ w_out
    x = layer_norm(DEEPNORM_ALPHA * x + gate1 * y, ln1_g, ln1_b)
    h = x * (1 + scale2) + shift2
    B, L, D = h.shape
    h2 = h.reshape(B * L, D)
    idx, wts = route(h2, w_router, router_bias)
    m = swiglu(h2, w_sh_gate, w_sh_up, w_sh_down) + routed_experts(h2, idx, wts, w_exp_gate, w_exp_up, w_exp_down)
    x = layer_norm(DEEPNORM_ALPHA * x + gate2 * m.reshape(B, L, D), ln2_g, ln2_b)
    return x, states


def setup_inputs(seed: int = 0) -> dict:
    key = jax.random.key(seed)
    ks = jax.random.split(key, 26)
    f32 = jnp.float32
    D = D_MODEL
    wb = min(WINDOW, PAST_LEN)

    def nrm(k, shape, s):
        return jax.random.normal(k, shape, f32) * s

    v0 = ATTN_WIDTH + KV_WIDTH
    w_in = nrm(ks[9], (DEPTH, D, IN_WIDTH), D ** -0.5)
    w_in = w_in.at[:, :, v0:v0 + KV_WIDTH].multiply(DEEPNORM_BETA)
    return {
        'x_prompt': nrm(ks[0], (BATCH, SEQ, D), 1.0),
        'x_sample': nrm(ks[1], (DEC_BATCH, DEC_SEQ, D), 1.0),
        'cache_k': nrm(ks[2], (DEPTH, DEC_BATCH, wb, N_KV_HEADS, HEAD_DIM), 1.0),
        'cache_v': nrm(ks[3], (DEPTH, DEC_BATCH, wb, N_KV_HEADS, HEAD_DIM), DEEPNORM_BETA),
        'state_pool': nrm(ks[4], (DEPTH, DEC_BATCH, POOL_HALO, POOL_WIDTH), 1.0),
        'c_prompt': nrm(ks[5], (BATCH, D), 1.0),
        'c_sample': nrm(ks[6], (DEC_BATCH, D), 1.0),
        'w_ada': nrm(ks[7], (DEPTH, D, 6 * D), 0.5 * D ** -0.5),
        'b_ada': nrm(ks[8], (DEPTH, 6 * D), 0.1),
        'w_in': w_in,
        'attn_sinks': nrm(ks[10], (DEPTH, N_HEADS), 1.0),
        'w_pool': nrm(ks[11], (DEPTH, N_POOL_GROUPS, POOL_GROUP_DIM, POOL_GROUP_DIM), POOL_GROUP_DIM ** -0.5),
        'pool_scale': 1.0 + nrm(ks[12], (DEPTH, POOL_WIDTH), 0.05),
        'w_out': nrm(ks[13], (DEPTH, D, D), DEEPNORM_BETA * D ** -0.5),
        'ln1_g': 1.0 + nrm(ks[14], (DEPTH, D), 0.05),
        'ln1_b': nrm(ks[15], (DEPTH, D), 0.02),
        'w_router': nrm(ks[16], (DEPTH, D, N_EXPERTS), D ** -0.5),
        'router_bias': nrm(ks[17], (DEPTH, N_EXPERTS), 0.01),
        'w_exp_gate': nrm(ks[18], (DEPTH, N_EXPERTS, D, EXPERT_DIM), D ** -0.5),
        'w_exp_up': nrm(ks[19], (DEPTH, N_EXPERTS, D, EXPERT_DIM), D ** -0.5),
        'w_exp_down': nrm(ks[20], (DEPTH, N_EXPERTS, EXPERT_DIM, D), DEEPNORM_BETA * EXPERT_DIM ** -0.5),
        'w_sh_gate': nrm(ks[21], (DEPTH, D, SHARED_DIM), D ** -0.5),
        'w_sh_up': nrm(ks[22], (DEPTH, D, SHARED_DIM), D ** -0.5),
        'w_sh_down': nrm(ks[23], (DEPTH, SHARED_DIM, D), DEEPNORM_BETA * SHARED_DIM ** -0.5),
        'ln2_g': 1.0 + nrm(ks[24], (DEPTH, D), 0.05),
        'ln2_b': nrm(ks[25], (DEPTH, D), 0.02),
    }


def reference(x_prompt, x_sample, cache_k, cache_v, state_pool, c_prompt, c_sample,
              w_ada, b_ada, w_in, attn_sinks, w_pool, pool_scale, w_out, ln1_g, ln1_b,
              w_router, router_bias, w_exp_gate, w_exp_up, w_exp_down,
              w_sh_gate, w_sh_up, w_sh_down, ln2_g, ln2_b):
    yp, ys = x_prompt, x_sample
    pos_p = jnp.arange(x_prompt.shape[1])
    pos_s = PAST_LEN + jnp.arange(x_sample.shape[1])
    kp, vp, pp, ksm, vsm, psm = [], [], [], [], [], []
    for l in range(DEPTH):
        shared = (w_ada[l], b_ada[l], w_in[l], w_out[l], ln1_g[l], ln1_b[l],
                  w_router[l], router_bias[l], w_exp_gate[l], w_exp_up[l], w_exp_down[l],
                  w_sh_gate[l], w_sh_up[l], w_sh_down[l], ln2_g[l], ln2_b[l])
        core_p = functools.partial(prompt_mixer, sinks=attn_sinks[l], w_pool=w_pool[l], pool_scale=pool_scale[l])
        core_s = functools.partial(sample_mixer, cache_k=cache_k[l], cache_v=cache_v[l], state_pool=state_pool[l],
                                   sinks=attn_sinks[l], w_pool=w_pool[l], pool_scale=pool_scale[l])
        yp, (k1, v1, p1) = decoder_layer(yp, c_prompt, pos_p, core_p, *shared)
        ys, (k2, v2, p2) = decoder_layer(ys, c_sample, pos_s, core_s, *shared)
        kp.append(k1); vp.append(v1); pp.append(p1)
        ksm.append(k2); vsm.append(v2); psm.append(p2)
    k_win_prompt = jnp.stack(kp)
    v_win_prompt = jnp.stack(vp)
    pool_prompt = jnp.stack(pp)
    k_win_sample = jnp.stack(ksm)
    v_win_sample = jnp.stack(vsm)
    pool_sample = jnp.stack(psm)
    return (yp, ys, k_win_prompt, v_win_prompt, pool_prompt, k_win_sample, v_win_sample, pool_sample)
```

```python
import functools

import jax
import jax.numpy as jnp
from jax import lax
from jax.experimental import pallas as pl
from jax.experimental.pallas import tpu as pltpu

F32, BF16, I32, U32 = jnp.float32, jnp.bfloat16, jnp.int32, jnp.uint32

D_MODEL = 1024
N_HEADS, HEAD_DIM, N_KV_HEADS = 16, 64, 4
KV_WIDTH = N_KV_HEADS * HEAD_DIM
WINDOW = 128
ROPE_THETA, ROT_DIM = 500000.0, 16
PAST_LEN = 16384
POOL_WINDOWS = (2, 4, 8, 16)
POOL_GROUP_DIM = 256
POOL_HALO = 15
N_EXPERTS, TOP_K, N_EXPERT_GROUPS, TOPK_GROUPS = 256, 8, 8, 4
EXPERTS_PER_GROUP = N_EXPERTS // N_EXPERT_GROUPS
EXPERT_DIM, SHARED_DIM = 256, 256
ROUTED_SCALE = 2.5
LN_EPS = 1e-5
DEPTH = 1
ALPHA = (2.0 * DEPTH) ** 0.25
Q0, K0, V0, U0, GA0, GB0, IN_WIDTH = 0, 1024, 1280, 1536, 2560, 3584, 4608

LANES = 128
VMEM_LIMIT_BYTES = 56 * 1024 * 1024

TQ = 256
HALO_ROWS = 16
SB = 16
TB = 128
BR = 256
NEG_INF = float("-inf")


def _cparams(n_axes=1):
    return pltpu.CompilerParams(dimension_semantics=("arbitrary",) * n_axes,
                                vmem_limit_bytes=VMEM_LIMIT_BYTES)


def _layer_norm(r, g, b):
    mu = jnp.mean(r, axis=-1, keepdims=True)
    xc = r - mu
    var = jnp.mean(xc * xc, axis=-1, keepdims=True)
    return xc * lax.rsqrt(var + LN_EPS) * g + b


def _silu(x):
    return x * jax.nn.sigmoid(x)


def _pack_bf16_pairs(x):
    c = x.shape[1] // 2
    bits = lax.bitcast_convert_type(x.astype(BF16).astype(F32), U32)
    return (bits[:, :c] >> 16) | (bits[:, c:] & jnp.uint32(0xFFFF0000))


def _unpack_bf16_pairs(w):
    lo = lax.bitcast_convert_type(w << 16, F32)
    hi = lax.bitcast_convert_type(w & jnp.uint32(0xFFFF0000), F32)
    return lo, hi


def _rope128(c, ra, rb, rc):
    return c * ra + pltpu.roll(c, LANES - 8, 1) * rb + pltpu.roll(c, 8, 1) * rc


def _head_variants(x2):
    lo = lax.broadcasted_iota(I32, (1, LANES), 1) < HEAD_DIM
    sw = pltpu.roll(x2, HEAD_DIM, 1)
    zero = jnp.zeros_like(x2)
    return (jnp.where(lo, x2, zero).astype(BF16), jnp.where(lo, zero, sw).astype(BF16),
            jnp.where(lo, sw, zero).astype(BF16), jnp.where(lo, zero, x2).astype(BF16))


def _softmax_sink(s, extra, sink):
    m = jnp.maximum(jnp.max(s, axis=1, keepdims=True), sink)
    if extra is not None:
        m = jnp.maximum(m, extra)
    p = jnp.exp(s - m)
    den = jnp.sum(p, axis=1, keepdims=True) + jnp.exp(sink - m)
    if extra is None:
        return p * (1.0 / den), None
    pe = jnp.exp(extra - m)
    r = 1.0 / (den + pe)
    return p * r, pe * r


def _ada_kernel(c_ref, w_ref, b_ref, o_ref):
    s = _silu(c_ref[...]).astype(BF16)
    o_ref[...] = jnp.dot(s, w_ref[...].astype(BF16), preferred_element_type=F32) + b_ref[...]


def _ada(c_all, w_ada, b_ada):
    rows, n = c_all.shape[0], w_ada.shape[1]
    tn = 1536
    return pl.pallas_call(
        _ada_kernel,
        out_shape=jax.ShapeDtypeStruct((rows, n), F32),
        grid=(n // tn,),
        in_specs=[pl.BlockSpec((rows, D_MODEL), lambda j: (0, 0)),
                  pl.BlockSpec((D_MODEL, tn), lambda j: (0, j)),
                  pl.BlockSpec((1, tn), lambda j: (0, j))],
        out_specs=pl.BlockSpec((rows, tn), lambda j: (0, j)),
        compiler_params=_cparams(),
        name="ada",
    )(c_all, w_ada, b_ada.reshape(1, n))


def _mixer_prompt_kernel(sink_ref, x_ref, mods_ref, ra_ref, rb_ref, rc_ref, win_ref, wpool_ref,
                         pscale_ref, wout_ref, g1_ref, b1_ref,
                         x1_ref, h2_ref, kwin_ref, vwin_ref, pst_ref,
                         q_s, kk_s, vv_s, u_s, attn_s, pool_s):
    i = pl.program_id(0)
    last = pl.num_programs(0) - 1

    @pl.when(i == 0)
    def _():
        kk_s[:, :, 0:WINDOW, :] = jnp.zeros((N_KV_HEADS, 2, WINDOW, LANES), BF16)
        vv_s[:, :, 0:WINDOW, :] = jnp.zeros((N_KV_HEADS, 2, WINDOW, LANES), BF16)
        u_s[0:HALO_ROWS, :] = jnp.zeros((HALO_ROWS, D_MODEL), F32)

    shift1 = mods_ref[:, 0:1024]
    scale1 = mods_ref[:, 1024:2048]
    gate1 = mods_ref[:, 2048:3072]
    shift2 = mods_ref[:, 3072:4096]
    scale2 = mods_ref[:, 4096:5120]

    x = x_ref[...]
    h = (x * (1.0 + scale1) + shift1).astype(BF16)
    ra, rb, rc = ra_ref[...], rb_ref[...], rc_ref[...]

    q = jnp.dot(h, win_ref[:, Q0:K0], preferred_element_type=F32)
    for c in range(D_MODEL // LANES):
        qc = _rope128(q[:, c * LANES:(c + 1) * LANES], ra, rb, rc)
        q_s[:, c * LANES:(c + 1) * LANES] = (qc * (HEAD_DIM ** -0.5)).astype(BF16)

    k = jnp.dot(h, win_ref[:, K0:V0], preferred_element_type=F32)
    v = jnp.dot(h, win_ref[:, V0:U0], preferred_element_type=F32)
    for c in range(KV_WIDTH // LANES):
        kr = _rope128(k[:, c * LANES:(c + 1) * LANES], ra, rb, rc)
        vr = v[:, c * LANES:(c + 1) * LANES]

        @pl.when(i == last)
        def _(kr=kr, vr=vr, c=c):
            kwin_ref[:, c * LANES:(c + 1) * LANES] = kr[TQ - WINDOW:, :]
            vwin_ref[:, c * LANES:(c + 1) * LANES] = vr[TQ - WINDOW:, :]

        ka, kb, kc_, kd = _head_variants(kr)
        va, vb, vc_, vd = _head_variants(vr)
        kk_s[2 * c, 0, WINDOW:, :] = ka
        kk_s[2 * c, 1, WINDOW:, :] = kb
        kk_s[2 * c + 1, 0, WINDOW:, :] = kc_
        kk_s[2 * c + 1, 1, WINDOW:, :] = kd
        vv_s[2 * c, 0, WINDOW:, :] = va
        vv_s[2 * c, 1, WINDOW:, :] = vb
        vv_s[2 * c + 1, 0, WINDOW:, :] = vc_
        vv_s[2 * c + 1, 1, WINDOW:, :] = vd

    u_s[HALO_ROWS:, :] = jnp.dot(h, win_ref[:, U0:GA0], preferred_element_type=F32)

    @pl.when(i == last)
    def _():
        pst_ref[...] = u_s[HALO_ROWS + TQ - POOL_HALO:, :]

    rowpos = i * TQ + lax.broadcasted_iota(I32, (TQ, 1), 0)
    for g, w in enumerate(POOL_WINDOWS):
        cs = slice(g * POOL_GROUP_DIM, (g + 1) * POOL_GROUP_DIM)
        acc = u_s[HALO_ROWS:, cs]
        for m in range(1, w):
            acc = acc + u_s[HALO_ROWS - m:HALO_ROWS - m + TQ, cs]
        cnt = jnp.minimum(w, rowpos + 1).astype(F32)
        d = acc / cnt - u_s[HALO_ROWS:, cs]
        y = jnp.dot(d.astype(BF16), wpool_ref[g], preferred_element_type=F32)
        pool_s[:, cs] = y * pscale_ref[:, cs]

    a_idx = lax.broadcasted_iota(I32, (WINDOW, 4 * WINDOW), 0)
    j_idx = lax.broadcasted_iota(I32, (WINDOW, 4 * WINDOW), 1) & (2 * WINDOW - 1)
    no_prev = jnp.where(i > 0, 0, 4 * WINDOW)
    for qb in range(TQ // WINDOW):
        rows = slice(qb * WINDOW, (qb + 1) * WINDOW)
        band = slice(qb * WINDOW, (qb + 2) * WINDOW)
        a_prev = a_idx + no_prev if qb == 0 else a_idx
        mask = jnp.logical_or(jnp.logical_and(j_idx < WINDOW, j_idx >= a_prev),
                              jnp.logical_and(j_idx >= WINDOW, (j_idx - WINDOW) <= a_idx))
        for g in range(N_KV_HEADS):
            k2 = jnp.concatenate([kk_s[g, 0, band, :], kk_s[g, 1, band, :]], axis=0)
            v2 = jnp.concatenate([vv_s[g, 0, band, :], vv_s[g, 1, band, :]], axis=0)
            for p in range(2):
                c = 2 * g + p
                q2 = q_s[rows, c * LANES:(c + 1) * LANES]
                s2 = lax.dot_general(q2, k2, (((1,), (1,)), ((), ())), preferred_element_type=F32)
                s2 = jnp.where(mask, s2, NEG_INF)
                pa, _ = _softmax_sink(s2[:, :2 * WINDOW], None, sink_ref[2 * c])
                pb, _ = _softmax_sink(s2[:, 2 * WINDOW:], None, sink_ref[2 * c + 1])
                p2 = jnp.concatenate([pa, pb], axis=1).astype(BF16)
                attn_s[rows, c * LANES:(c + 1) * LANES] = jnp.dot(p2, v2, preferred_element_type=F32)

    kk_s[:, :, 0:WINDOW, :] = kk_s[:, :, TQ:, :]
    vv_s[:, :, 0:WINDOW, :] = vv_s[:, :, TQ:, :]
    u_s[0:HALO_ROWS, :] = u_s[TQ:, :]

    ga = jnp.dot(h, win_ref[:, GA0:GB0], preferred_element_type=F32)
    gb = jnp.dot(h, win_ref[:, GB0:IN_WIDTH], preferred_element_type=F32)
    merged = jax.nn.sigmoid(ga) * attn_s[...] + jax.nn.sigmoid(gb) * pool_s[...]
    y = jnp.dot(merged.astype(BF16), wout_ref[...], preferred_element_type=F32)
    x1 = _layer_norm(ALPHA * x + gate1 * y, g1_ref[...], b1_ref[...])
    x1_ref[...] = x1
    h2_ref[...] = _pack_bf16_pairs(x1 * (1.0 + scale2) + shift2)


def _mixer_prompt(x, mods_p, tabs, w_in_b, sinks, w_pool_b, pool_scale, w_out_b, ln1_g, ln1_b):
    t = x.shape[0]
    full = lambda shape: pl.BlockSpec(shape, lambda i: (0,) * len(shape))
    rowblk = lambda width: pl.BlockSpec((TQ, width), lambda i: (i, 0))
    return pl.pallas_call(
        _mixer_prompt_kernel,
        out_shape=(jax.ShapeDtypeStruct((t, D_MODEL), F32),
                   jax.ShapeDtypeStruct((t, D_MODEL // 2), U32),
                   jax.ShapeDtypeStruct((WINDOW, KV_WIDTH), F32),
                   jax.ShapeDtypeStruct((WINDOW, KV_WIDTH), F32),
                   jax.ShapeDtypeStruct((POOL_HALO, D_MODEL), F32)),
        grid=(t // TQ,),
        in_specs=[pl.BlockSpec(memory_space=pltpu.SMEM),
                  rowblk(D_MODEL), full((1, 6 * D_MODEL)),
                  rowblk(LANES), rowblk(LANES), rowblk(LANES),
                  full((D_MODEL, IN_WIDTH)), full((4, POOL_GROUP_DIM, POOL_GROUP_DIM)),
                  full((1, D_MODEL)), full((D_MODEL, D_MODEL)), full((1, D_MODEL)), full((1, D_MODEL))],
        out_specs=(rowblk(D_MODEL), rowblk(D_MODEL // 2),
                   full((WINDOW, KV_WIDTH)), full((WINDOW, KV_WIDTH)), full((POOL_HALO, D_MODEL))),
        scratch_shapes=[pltpu.VMEM((TQ, D_MODEL), BF16),
                        pltpu.VMEM((N_KV_HEADS, 2, WINDOW + TQ, LANES), BF16),
                        pltpu.VMEM((N_KV_HEADS, 2, WINDOW + TQ, LANES), BF16),
                        pltpu.VMEM((HALO_ROWS + TQ, D_MODEL), F32),
                        pltpu.VMEM((TQ, D_MODEL), F32),
                        pltpu.VMEM((TQ, D_MODEL), F32)],
        compiler_params=_cparams(),
        name="mixer_prompt",
    )(sinks, x, mods_p, *tabs, w_in_b, w_pool_b, pool_scale, w_out_b, ln1_g, ln1_b)


def _sample_proj_kernel(x_ref, mods_ref, ra_ref, rb_ref, rc_ref, win_ref, spt_ref, wpool_ref, pscale_ref,
                        q_ref, k_ref, v_ref, u_ref, pool_ref, ga_ref, gb_ref):
    x = x_ref[...]
    h = (x * (1.0 + mods_ref[:, 1024:2048]) + mods_ref[:, 0:1024]).astype(BF16)
    ra, rb, rc = ra_ref[...], rb_ref[...], rc_ref[...]
    q = jnp.dot(h, win_ref[:, Q0:K0], preferred_element_type=F32)
    for c in range(D_MODEL // LANES):
        qc = _rope128(q[:, c * LANES:(c + 1) * LANES], ra, rb, rc)
        q_ref[:, c * LANES:(c + 1) * LANES] = (qc * (HEAD_DIM ** -0.5)).astype(BF16)
    k = jnp.dot(h, win_ref[:, K0:V0], preferred_element_type=F32)
    for c in range(KV_WIDTH // LANES):
        k_ref[:, c * LANES:(c + 1) * LANES] = _rope128(k[:, c * LANES:(c + 1) * LANES], ra, rb, rc)
    v_ref[...] = jnp.dot(h, win_ref[:, V0:U0], preferred_element_type=F32)
    u = jnp.dot(h, win_ref[:, U0:GA0], preferred_element_type=F32)
    u_ref[...] = u
    for g, w in enumerate(POOL_WINDOWS):
        cs = slice(g * POOL_GROUP_DIM, (g + 1) * POOL_GROUP_DIM)
        acc = u[:, cs]
        for m in range(1, w):
            acc = acc + spt_ref[POOL_HALO - m, :, cs]
        d = acc / float(min(w, PAST_LEN + 1)) - u[:, cs]
        y = jnp.dot(d.astype(BF16), wpool_ref[g], preferred_element_type=F32)
        pool_ref[:, cs] = y * pscale_ref[:, cs]
    ga_ref[...] = jnp.dot(h, win_ref[:, GA0:GB0], preferred_element_type=F32)
    gb_ref[...] = jnp.dot(h, win_ref[:, GB0:IN_WIDTH], preferred_element_type=F32)


def _sample_proj(x_s, mods_s, tabs, w_in_b, sp_t, w_pool_b, pool_scale):
    b = x_s.shape[0]
    sd = lambda w, dt=F32: jax.ShapeDtypeStruct((b, w), dt)
    return pl.pallas_call(
        _sample_proj_kernel,
        out_shape=(sd(D_MODEL, BF16), sd(KV_WIDTH), sd(KV_WIDTH), sd(D_MODEL), sd(D_MODEL),
                   sd(D_MODEL), sd(D_MODEL)),
        compiler_params=pltpu.CompilerParams(vmem_limit_bytes=VMEM_LIMIT_BYTES),
        name="sample_proj",
    )(x_s, mods_s, *tabs, w_in_b, sp_t, w_pool_b, pool_scale)


def _sample_attn_kernel(sink_ref, q_ref, ck_ref, cv_ref, kn_ref, vn_ref, o_ref):
    nkeys = SB * WINDOW
    lo = lax.broadcasted_iota(I32, (1, LANES), 1) < HEAD_DIM
    row_b = lax.broadcasted_iota(I32, (2 * SB, 2 * nkeys), 0) & (SB - 1)
    col_b = (lax.broadcasted_iota(I32, (2 * SB, 2 * nkeys), 1) & (nkeys - 1)) >> 7
    own = row_b == col_b
    first_pair = lax.broadcasted_iota(I32, (2 * SB, 1), 0) < SB
    for c in range(KV_WIDTH // LANES):
        kvar = _head_variants(ck_ref[:, c * LANES:(c + 1) * LANES])
        vvar = _head_variants(cv_ref[:, c * LANES:(c + 1) * LANES])
        kn = kn_ref[:, c * LANES:(c + 1) * LANES].astype(BF16).astype(F32)
        vn = vn_ref[:, c * LANES:(c + 1) * LANES].astype(BF16).astype(F32)
        kn_sw, vn_sw = pltpu.roll(kn, HEAD_DIM, 1), pltpu.roll(vn, HEAD_DIM, 1)
        for e in range(2):
            g = 2 * c + e
            k2 = jnp.concatenate([kvar[2 * e], kvar[2 * e + 1]], axis=0)
            v2 = jnp.concatenate([vvar[2 * e], vvar[2 * e + 1]], axis=0)
            kdup = jnp.where(lo, kn, kn_sw) if e == 0 else jnp.where(lo, kn_sw, kn)
            vdup = jnp.where(lo, vn, vn_sw) if e == 0 else jnp.where(lo, vn_sw, vn)
            qa = q_ref[:, (2 * g) * LANES:(2 * g + 1) * LANES]
            qb = q_ref[:, (2 * g + 1) * LANES:(2 * g + 2) * LANES]
            q4 = jnp.concatenate([qa, qb], axis=0)
            s = lax.dot_general(q4, k2, (((1,), (1,)), ((), ())), preferred_element_type=F32)
            s = jnp.where(own, s, NEG_INF)
            prod = q4.astype(F32) * jnp.concatenate([kdup, kdup], axis=0)
            s_new = (jnp.sum(jnp.where(lo, prod, 0.0), axis=1, keepdims=True),
                     jnp.sum(jnp.where(lo, 0.0, prod), axis=1, keepdims=True))
            ps, pn = [], []
            for hh in range(2):
                sink = jnp.where(first_pair, sink_ref[4 * g + hh], sink_ref[4 * g + 2 + hh])
                p_c, p_n = _softmax_sink(s[:, hh * nkeys:(hh + 1) * nkeys], s_new[hh], sink)
                ps.append(p_c)
                pn.append(p_n)
            p2 = jnp.concatenate(ps, axis=1).astype(BF16)
            o = jnp.dot(p2, v2, preferred_element_type=F32)
            o = o + jnp.where(lo, pn[0], pn[1]) * jnp.concatenate([vdup, vdup], axis=0)
            o_ref[:, (2 * g) * LANES:(2 * g + 1) * LANES] = o[:SB]
            o_ref[:, (2 * g + 1) * LANES:(2 * g + 2) * LANES] = o[SB:]


def _sample_attn(sinks, q, ck2d, cv2d, k_new, v_new):
    b = q.shape[0]
    return pl.pallas_call(
        _sample_attn_kernel,
        out_shape=jax.ShapeDtypeStruct((b, D_MODEL), F32),
        grid=(b // SB,),
        in_specs=[pl.BlockSpec(memory_space=pltpu.SMEM),
                  pl.BlockSpec((SB, D_MODEL), lambda i: (i, 0)),
                  pl.BlockSpec((SB * WINDOW, KV_WIDTH), lambda i: (i, 0)),
                  pl.BlockSpec((SB * WINDOW, KV_WIDTH), lambda i: (i, 0)),
                  pl.BlockSpec((SB, KV_WIDTH), lambda i: (i, 0)),
                  pl.BlockSpec((SB, KV_WIDTH), lambda i: (i, 0))],
        out_specs=pl.BlockSpec((SB, D_MODEL), lambda i: (i, 0)),
        compiler_params=_cparams(),
        name="sample_attn",
    )(sinks, q, ck2d, cv2d, k_new, v_new)


def _sample_merge_kernel(x_ref, mods_ref, attn_ref, pool_ref, ga_ref, gb_ref, wout_ref, g1_ref, b1_ref,
                         x1_ref, h2_ref):
    merged = jax.nn.sigmoid(ga_ref[...]) * attn_ref[...] + jax.nn.sigmoid(gb_ref[...]) * pool_ref[...]
    y = jnp.dot(merged.astype(BF16), wout_ref[...], preferred_element_type=F32)
    x1 = _layer_norm(ALPHA * x_ref[...] + mods_ref[:, 2048:3072] * y, g1_ref[...], b1_ref[...])
    x1_ref[...] = x1
    h2_ref[...] = _pack_bf16_pairs(x1 * (1.0 + mods_ref[:, 4096:5120]) + mods_ref[:, 3072:4096])


def _sample_merge(x_s, mods_s, attn, pool, ga, gb, w_out_b, ln1_g, ln1_b):
    b = x_s.shape[0]
    return pl.pallas_call(
        _sample_merge_kernel,
        out_shape=(jax.ShapeDtypeStruct((b, D_MODEL), F32), jax.ShapeDtypeStruct((b, D_MODEL // 2), U32)),
        compiler_params=pltpu.CompilerParams(vmem_limit_bytes=VMEM_LIMIT_BYTES),
        name="sample_merge",
    )(x_s, mods_s, attn, pool, ga, gb, w_out_b, ln1_g, ln1_b)


def _router_kernel(h2_ref, wrt_ref, bias_ref, idx_ref, wts_ref, rank_ref, cnt_ref, carry_s):
    i = pl.program_id(0)

    @pl.when(i == 0)
    def _():
        carry_s[...] = jnp.zeros_like(carry_s)

    lo, hi = _unpack_bf16_pairs(h2_ref[...])
    nt = (((1,), (1,)), ((), ()))
    half = D_MODEL // 2
    logits = (lax.dot_general(wrt_ref[:, :half], lo.astype(BF16), nt, preferred_element_type=F32)
              + lax.dot_general(wrt_ref[:, half:], hi.astype(BF16), nt, preferred_element_type=F32))
    scores = jax.nn.sigmoid(logits)
    sel = scores + bias_ref[...]

    gi = lax.broadcasted_iota(I32, (EXPERTS_PER_GROUP, TB), 0).astype(F32)
    gscore = []
    for g in range(N_EXPERT_GROUPS):
        blk = sel[g * EXPERTS_PER_GROUP:(g + 1) * EXPERTS_PER_GROUP, :]
        m1 = jnp.max(blk, axis=0, keepdims=True)
        f1 = jnp.min(jnp.where(blk == m1, gi, float(EXPERTS_PER_GROUP)), axis=0, keepdims=True)
        m2 = jnp.max(jnp.where(gi == f1, NEG_INF, blk), axis=0, keepdims=True)
        gscore.append(m1 + m2)

    gsel = [jnp.zeros((1, TB), jnp.bool_) for _ in range(N_EXPERT_GROUPS)]
    for _ in range(TOPK_GROUPS):
        m = gscore[0]
        for g in range(1, N_EXPERT_GROUPS):
            m = jnp.maximum(m, gscore[g])
        found = jnp.zeros((1, TB), jnp.bool_)
        for g in range(N_EXPERT_GROUPS):
            pick = jnp.logical_and(gscore[g] == m, jnp.logical_not(found))
            found = jnp.logical_or(found, pick)
            gsel[g] = jnp.logical_or(gsel[g], pick)
            gscore[g] = jnp.where(pick, NEG_INF, gscore[g])
    cur = jnp.concatenate(
        [jnp.where(gsel[g], sel[g * EXPERTS_PER_GROUP:(g + 1) * EXPERTS_PER_GROUP, :], NEG_INF)
         for g in range(N_EXPERT_GROUPS)], axis=0)

    ei = lax.broadcasted_iota(I32, (N_EXPERTS, TB), 0).astype(F32)
    onehot = jnp.zeros((N_EXPERTS, TB), F32)
    picks, raw_w = [], []
    for _ in range(TOP_K):
        m = jnp.max(cur, axis=0, keepdims=True)
        f = jnp.min(jnp.where(cur == m, ei, float(N_EXPERTS)), axis=0, keepdims=True)
        pick = ei == f
        picks.append(f)
        raw_w.append(jnp.sum(jnp.where(pick, scores, 0.0), axis=0, keepdims=True))
        cur = jnp.where(pick, NEG_INF, cur)
        onehot = onehot + pick.astype(F32)
    wsum = raw_w[0]
    for k in range(1, TOP_K):
        wsum = wsum + raw_w[k]

    tri = (lax.broadcasted_iota(I32, (TB, TB), 0) <= lax.broadcasted_iota(I32, (TB, TB), 1)).astype(BF16)
    incl = jnp.dot(onehot.astype(BF16), tri, preferred_element_type=F32)
    pos = carry_s[...] + incl - onehot
    for k in range(TOP_K):
        idx_ref[k:k + 1, :] = picks[k].astype(I32)
        wts_ref[k:k + 1, :] = raw_w[k] / wsum * ROUTED_SCALE
        rank_ref[k:k + 1, :] = jnp.sum(jnp.where(ei == picks[k], pos, 0.0), axis=0, keepdims=True).astype(I32)
    carry = carry_s[...] + jnp.sum(onehot, axis=1, keepdims=True)
    carry_s[...] = carry
    cnt_ref[...] = carry


def _router(h2_all, w_router_t_b, router_bias):
    t_all = h2_all.shape[0]
    tok = lambda dt: jax.ShapeDtypeStruct((TOP_K, t_all), dt)
    return pl.pallas_call(
        _router_kernel,
        out_shape=(tok(I32), tok(F32), tok(I32), jax.ShapeDtypeStruct((N_EXPERTS, LANES), F32)),
        grid=(t_all // TB,),
        in_specs=[pl.BlockSpec((TB, D_MODEL // 2), lambda i: (i, 0)),
                  pl.BlockSpec((N_EXPERTS, D_MODEL), lambda i: (0, 0)),
                  pl.BlockSpec((N_EXPERTS, 1), lambda i: (0, 0))],
        out_specs=(pl.BlockSpec((TOP_K, TB), lambda i: (0, i)),
                   pl.BlockSpec((TOP_K, TB), lambda i: (0, i)),
                   pl.BlockSpec((TOP_K, TB), lambda i: (0, i)),
                   pl.BlockSpec((N_EXPERTS, LANES), lambda i: (0, 0))),
        scratch_shapes=[pltpu.VMEM((N_EXPERTS, LANES), F32)],
        compiler_params=_cparams(),
        name="router",
    )(h2_all, w_router_t_b, router_bias.reshape(N_EXPERTS, 1))


def _moe_meta_kernel(idx_ref, rank_ref, cnt_ref, dest_ref, bmap_ref, bstart_s, *, nbp):
    i = pl.program_id(0)

    @pl.when(i == 0)
    def _():
        nb = jnp.floor((cnt_ref[...] + float(BR - 1)) * (1.0 / BR))
        e_r = lax.broadcasted_iota(I32, (N_EXPERTS, N_EXPERTS), 0)
        e_c = lax.broadcasted_iota(I32, (N_EXPERTS, N_EXPERTS), 1)
        bstart = jnp.dot((e_c < e_r).astype(BF16), nb.astype(BF16), preferred_element_type=F32)
        bstart_s[...] = bstart
        bend = bstart[:, 0:1] + nb[:, 0:1]
        nused = jnp.max(bend, axis=0, keepdims=True)
        b_i = lax.broadcasted_iota(I32, (1, nbp), 1).astype(F32)
        raw = jnp.sum((bend <= b_i).astype(F32), axis=0, keepdims=True)
        last_e = jnp.sum((bend <= nused - 1.0).astype(F32), axis=0, keepdims=True)
        starts = jnp.logical_and(bstart[:, 0:1] == b_i, nb[:, 0:1] > 0.0)
        first = jnp.max(starts.astype(F32), axis=0, keepdims=True)
        bmap_ref[0:1, :] = jnp.where(b_i < nused, raw, last_e).astype(I32)
        bmap_ref[1:2, :] = first.astype(I32)
        bmap_ref[2:3, :] = jnp.broadcast_to(nused, (1, nbp)).astype(I32)
        bmap_ref[3:8, :] = jnp.zeros((5, nbp), I32)

    ei = lax.broadcasted_iota(I32, (N_EXPERTS, TB), 0)
    bstart = bstart_s[...]
    for k in range(TOP_K):
        st = jnp.sum(jnp.where(ei == idx_ref[k:k + 1, :], bstart, 0.0), axis=0, keepdims=True)
        dest_ref[k:k + 1, :] = st.astype(I32) * BR + rank_ref[k:k + 1, :]


def _moe_meta(idx, rank, counts, nbp):
    t_all = idx.shape[1]
    return pl.pallas_call(
        functools.partial(_moe_meta_kernel, nbp=nbp),
        out_shape=(jax.ShapeDtypeStruct((TOP_K, t_all), I32), jax.ShapeDtypeStruct((8, nbp), I32)),
        grid=(t_all // TB,),
        in_specs=[pl.BlockSpec((TOP_K, TB), lambda i: (0, i)),
                  pl.BlockSpec((TOP_K, TB), lambda i: (0, i)),
                  pl.BlockSpec((N_EXPERTS, LANES), lambda i: (0, 0))],
        out_specs=(pl.BlockSpec((TOP_K, TB), lambda i: (0, i)),
                   pl.BlockSpec((8, nbp), lambda i: (0, 0))),
        scratch_shapes=[pltpu.VMEM((N_EXPERTS, LANES), F32)],
        compiler_params=_cparams(),
        name="moe_meta",
    )(idx, rank, counts)


def _dispatch_kernel(dest_ref, h2_ref, xs_in_ref, xs_ref, sem):
    del xs_in_ref

    def row_copy(t, d):
        return pltpu.make_async_copy(h2_ref.at[pl.ds(t, 1), :], xs_ref.at[pl.ds(d, 1), :], sem)

    def start(t, carry):
        for k in range(TOP_K):
            row_copy(t, dest_ref[0, 0, k * TB + t]).start()
        return carry

    lax.fori_loop(0, TB, start, 0)

    def wait(t, carry):
        for k in range(TOP_K):
            row_copy(t, dest_ref[0, 0, k * TB + t]).wait()
        return carry

    lax.fori_loop(0, TB, wait, 0)


def _dispatch(dest_blk, h2_all, n_rows):
    t_all = h2_all.shape[0]
    xs0 = jnp.zeros((n_rows, D_MODEL // 2), U32)
    return pl.pallas_call(
        _dispatch_kernel,
        out_shape=jax.ShapeDtypeStruct((n_rows, D_MODEL // 2), U32),
        grid=(t_all // TB,),
        in_specs=[pl.BlockSpec((1, 1, TOP_K * TB), lambda i: (i, 0, 0), memory_space=pltpu.SMEM),
                  pl.BlockSpec((TB, D_MODEL // 2), lambda i: (i, 0)),
                  pl.BlockSpec(memory_space=pl.ANY)],
        out_specs=pl.BlockSpec(memory_space=pl.ANY),
        scratch_shapes=[pltpu.SemaphoreType.DMA(())],
        input_output_aliases={2: 0},
        compiler_params=_cparams(),
        name="dispatch",
    )(dest_blk, h2_all, xs0)


def _experts_kernel(bmap_ref, xs_ref, wg_ref, wu_ref, wd_ref, ys_ref, wgu_s, wd_s, *, nbp):
    b = pl.program_id(0)

    @pl.when(b < bmap_ref[2 * nbp])
    def _():
        @pl.when(bmap_ref[nbp + b] == 1)
        def _():
            wgu_s[:, :EXPERT_DIM] = wg_ref[...].astype(BF16)
            wgu_s[:, EXPERT_DIM:] = wu_ref[...].astype(BF16)
            wd_s[...] = wd_ref[...].astype(BF16)

        lo, hi = _unpack_bf16_pairs(xs_ref[...])
        half = D_MODEL // 2
        gu = (jnp.dot(lo.astype(BF16), wgu_s[:half, :], preferred_element_type=F32)
              + jnp.dot(hi.astype(BF16), wgu_s[half:, :], preferred_element_type=F32))
        act = _silu(gu[:, :EXPERT_DIM]) * gu[:, EXPERT_DIM:]
        y = jnp.dot(act.astype(BF16), wd_s[...], preferred_element_type=F32)
        ys_ref[...] = _pack_bf16_pairs(y)

    @pl.when(b >= bmap_ref[2 * nbp])
    def _():
        ys_ref[...] = jnp.zeros_like(ys_ref)


def _experts(bmap_flat, xs, w_gate, w_up, w_down, nbp, n_blocks):
    row_map = lambda b, m: (jnp.minimum(b, m[2 * nbp] - 1), 0)
    w_map = lambda b, m: (m[b], 0, 0)
    grid_spec = pltpu.PrefetchScalarGridSpec(
        num_scalar_prefetch=1,
        grid=(n_blocks,),
        in_specs=[pl.BlockSpec((BR, D_MODEL // 2), row_map),
                  pl.BlockSpec((None, D_MODEL, EXPERT_DIM), w_map),
                  pl.BlockSpec((None, D_MODEL, EXPERT_DIM), w_map),
                  pl.BlockSpec((None, EXPERT_DIM, D_MODEL), w_map)],
        out_specs=pl.BlockSpec((BR, D_MODEL // 2), lambda b, m: (b, 0)),
        scratch_shapes=[pltpu.VMEM((D_MODEL, 2 * EXPERT_DIM), BF16),
                        pltpu.VMEM((EXPERT_DIM, D_MODEL), BF16)],
    )
    return pl.pallas_call(
        functools.partial(_experts_kernel, nbp=nbp),
        out_shape=jax.ShapeDtypeStruct(xs.shape, U32),
        grid_spec=grid_spec,
        compiler_params=_cparams(),
        name="experts",
    )(bmap_flat, xs, w_gate, w_up, w_down)


def _combine_kernel(dest_ref, x1_ref, mods_ref, wts_ref, ys_ref, wsgu_ref, wsd_ref, g2_ref, b2_ref,
                    out_ref, buf, sem):
    def row_copy(t, k, d):
        return pltpu.make_async_copy(ys_ref.at[pl.ds(d, 1), :], buf.at[k, pl.ds(t, 1), :], sem)

    def start(t, carry):
        for k in range(TOP_K):
            row_copy(t, k, dest_ref[0, 0, k * TB + t]).start()
        return carry

    lax.fori_loop(0, TB, start, 0)

    x1 = x1_ref[...]
    h2 = (x1 * (1.0 + mods_ref[:, 4096:5120]) + mods_ref[:, 3072:4096]).astype(BF16)
    gu = jnp.dot(h2, wsgu_ref[...], preferred_element_type=F32)
    act = _silu(gu[:, :SHARED_DIM]) * gu[:, SHARED_DIM:]
    shared = jnp.dot(act.astype(BF16), wsd_ref[...], preferred_element_type=F32)

    def wait(t, carry):
        for k in range(TOP_K):
            row_copy(t, k, dest_ref[0, 0, k * TB + t]).wait()
        return carry

    lax.fori_loop(0, TB, wait, 0)

    acc_lo = jnp.zeros((TB, D_MODEL // 2), F32)
    acc_hi = jnp.zeros((TB, D_MODEL // 2), F32)
    for k in range(TOP_K):
        lo, hi = _unpack_bf16_pairs(buf[k])
        wk = wts_ref[:, k:k + 1]
        acc_lo = acc_lo + wk * lo
        acc_hi = acc_hi + wk * hi
    m = shared + jnp.concatenate([acc_lo, acc_hi], axis=1)
    out_ref[...] = _layer_norm(ALPHA * x1 + mods_ref[:, 5120:6144] * m, g2_ref[...], b2_ref[...])


def _combine(dest_blk, x1, mods, wts_t, ys, wsgu_b, wsd_b, ln2_g, ln2_b, blk_off):
    t = x1.shape[0]
    per_row = mods.shape[0] != 1
    mods_spec = (pl.BlockSpec((TB, 6 * D_MODEL), lambda i: (i, 0)) if per_row
                 else pl.BlockSpec((1, 6 * D_MODEL), lambda i: (0, 0)))
    full = lambda shape: pl.BlockSpec(shape, lambda i: (0,) * len(shape))
    return pl.pallas_call(
        _combine_kernel,
        out_shape=jax.ShapeDtypeStruct((t, D_MODEL), F32),
        grid=(t // TB,),
        in_specs=[pl.BlockSpec((1, 1, TOP_K * TB), lambda i: (i + blk_off, 0, 0), memory_space=pltpu.SMEM),
                  pl.BlockSpec((TB, D_MODEL), lambda i: (i, 0)),
                  mods_spec,
                  pl.BlockSpec((TB, TOP_K), lambda i: (i + blk_off, 0)),
                  pl.BlockSpec(memory_space=pl.ANY),
                  full((D_MODEL, 2 * SHARED_DIM)), full((SHARED_DIM, D_MODEL)),
                  full((1, D_MODEL)), full((1, D_MODEL))],
        out_specs=pl.BlockSpec((TB, D_MODEL), lambda i: (i, 0)),
        scratch_shapes=[pltpu.VMEM((TOP_K, TB, D_MODEL // 2), U32), pltpu.SemaphoreType.DMA(())],
        compiler_params=_cparams(),
        name="combine",
    )(dest_blk, x1, mods, wts_t, ys, wsgu_b, wsd_b, ln2_g, ln2_b)


def _rope_tables(pos):
    half = ROT_DIM // 2
    inv = ROPE_THETA ** (-jnp.arange(0, ROT_DIM, 2, dtype=F32) / ROT_DIM)
    ang = pos.astype(F32)[:, None] * inv[None, :]
    cos, sin = jnp.cos(ang), jnp.sin(ang)
    n = pos.shape[0]
    ones = jnp.ones((n, HEAD_DIM - ROT_DIM), F32)
    z = lambda w: jnp.zeros((n, w), F32)
    ra = jnp.concatenate([cos, cos, ones], axis=1)
    rb = jnp.concatenate([-sin, z(HEAD_DIM - half)], axis=1)
    rc = jnp.concatenate([z(half), sin, z(HEAD_DIM - ROT_DIM)], axis=1)
    return tuple(jnp.tile(t, (1, LANES // HEAD_DIM)) for t in (ra, rb, rc))


def kernel(x_prompt, x_sample, cache_k, cache_v, state_pool, c_prompt, c_sample, w_ada, b_ada, w_in, attn_sinks, w_pool, pool_scale, w_out, ln1_g, ln1_b, w_router, router_bias, w_exp_gate, w_exp_up, w_exp_down, w_sh_gate, w_sh_up, w_sh_down, ln2_g, ln2_b):
    assert w_ada.shape[0] == DEPTH == 1 and x_prompt.shape[0] == 1 and x_sample.shape[1] == 1
    t_p, n_s = x_prompt.shape[1], x_sample.shape[0]
    assert t_p % TQ == 0 and n_s % SB == 0 and n_s == TB and cache_k.shape[2] == WINDOW
    t_all = t_p + n_s

    row = lambda a: a[0].reshape(1, -1)
    w_in_b, w_out_b, w_pool_b = w_in[0].astype(BF16), w_out[0].astype(BF16), w_pool[0].astype(BF16)
    wsgu_b = jnp.concatenate([w_sh_gate[0], w_sh_up[0]], axis=1).astype(BF16)
    wsd_b = w_sh_down[0].astype(BF16)
    w_router_t_b = w_router[0].T.astype(BF16)
    sinks = attn_sinks[0]
    pscale, g1, b1, g2, b2 = row(pool_scale), row(ln1_g), row(ln1_b), row(ln2_g), row(ln2_b)

    x_s = x_sample.reshape(n_s, D_MODEL)
    c_all = jnp.concatenate([c_sample, c_prompt, jnp.zeros((7, D_MODEL), F32)], axis=0)
    mods = _ada(c_all, w_ada[0], b_ada[0])
    mods_s, mods_p = mods[:n_s], mods[n_s:n_s + 1]

    x_p = x_prompt.reshape(t_p, D_MODEL)
    x1_p, h2_p, k_win, v_win, pool_st = _mixer_prompt(
        x_p, mods_p, _rope_tables(jnp.arange(t_p)), w_in_b, sinks, w_pool_b, pscale, w_out_b, g1, b1)

    sp_t = jnp.transpose(state_pool[0], (1, 0, 2))
    q_s, k_new, v_new, u_new, pool_s, ga_s, gb_s = _sample_proj(
        x_s, mods_s, _rope_tables(jnp.full((1,), PAST_LEN)), w_in_b, sp_t, w_pool_b, pscale)
    attn_s = _sample_attn(sinks, q_s, cache_k[0].reshape(n_s * WINDOW, KV_WIDTH),
                          cache_v[0].reshape(n_s * WINDOW, KV_WIDTH), k_new, v_new)
    x1_s, h2_s = _sample_merge(x_s, mods_s, attn_s, pool_s, ga_s, gb_s, w_out_b, g1, b1)

    h2_all = jnp.concatenate([h2_p, h2_s], axis=0)
    idx, wts, rank, counts = _router(h2_all, w_router_t_b, router_bias[0])
    n_blocks = t_all * TOP_K // BR + N_EXPERTS
    nbp = -(-n_blocks // LANES) * LANES
    dest, bmap = _moe_meta(idx, rank, counts, nbp)
    n_tb = t_all // TB
    dest_blk = dest.reshape(TOP_K, n_tb, TB).transpose(1, 0, 2).reshape(n_tb, 1, TOP_K * TB)
    xs = _dispatch(dest_blk, h2_all, n_blocks * BR)
    ys = _experts(bmap[:3].reshape(-1), xs, w_exp_gate[0], w_exp_up[0], w_exp_down[0], nbp, n_blocks)
    wts_t = wts.T
    y_p = _combine(dest_blk, x1_p, mods_p, wts_t, ys, wsgu_b, wsd_b, g2, b2, 0)
    y_s = _combine(dest_blk, x1_s, mods_s, wts_t, ys, wsgu_b, wsd_b, g2, b2, t_p // TB)

    k_win_s = jnp.concatenate([cache_k[0][:, 1:], k_new.reshape(n_s, 1, N_KV_HEADS, HEAD_DIM)], axis=1)
    v_win_s = jnp.concatenate([cache_v[0][:, 1:], v_new.reshape(n_s, 1, N_KV_HEADS, HEAD_DIM)], axis=1)
    pool_s_out = jnp.concatenate([state_pool[0][:, 1:], u_new[:, None, :]], axis=1)
    return (y_p.reshape(1, t_p, D_MODEL), y_s.reshape(n_s, 1, D_MODEL),
            k_win.reshape(1, 1, WINDOW, N_KV_HEADS, HEAD_DIM), v_win.reshape(1, 1, WINDOW, N_KV_HEADS, HEAD_DIM),
            pool_st.reshape(1, 1, POOL_HALO, D_MODEL),
            k_win_s[None], v_win_s[None], pool_s_out[None])
```

```python
import functools

import jax
import jax.numpy as jnp
from jax import lax
from jax.experimental import pallas as pl
from jax.experimental.pallas import tpu as pltpu

F32, BF16, I32, U32 = jnp.float32, jnp.bfloat16, jnp.int32, jnp.uint32

D_MODEL = 1024
N_HEADS, HEAD_DIM, N_KV_HEADS = 16, 64, 4
KV_WIDTH = N_KV_HEADS * HEAD_DIM
WINDOW = 128
ROPE_THETA, ROT_DIM = 500000.0, 16
PAST_LEN = 16384
POOL_WINDOWS = (2, 4, 8, 16)
POOL_GROUP_DIM = 256
POOL_HALO = 15
N_EXPERTS, TOP_K, N_EXPERT_GROUPS, TOPK_GROUPS = 256, 8, 8, 4
EXPERTS_PER_GROUP = N_EXPERTS // N_EXPERT_GROUPS
EXPERT_DIM, SHARED_DIM = 256, 256
ROUTED_SCALE = 2.5
LN_EPS = 1e-5
DEPTH = 1
ALPHA = (2.0 * DEPTH) ** 0.25
Q0, K0, V0, U0, GA0, GB0, IN_WIDTH = 0, 1024, 1280, 1536, 2560, 3584, 4608

LANES = 128
VMEM_LIMIT_BYTES = 56 * 1024 * 1024

TQ = 256
HALO_ROWS = 16
SB = 16
TB = 128
ROW_TILE = 8
CH = 128
CH_TILES = CH // ROW_TILE
CH_SHIFT = CH_TILES.bit_length() - 1
ZROWS = 256
TAIL_ROWS = -(-(N_EXPERTS * (ROW_TILE - 1) + CH) // ZROWS) * ZROWS
NEG_INF = float("-inf")


def _cparams(n_axes=1):
    return pltpu.CompilerParams(dimension_semantics=("arbitrary",) * n_axes,
                                vmem_limit_bytes=VMEM_LIMIT_BYTES)


def _layer_norm(r, g, b):
    mu = jnp.mean(r, axis=-1, keepdims=True)
    xc = r - mu
    var = jnp.mean(xc * xc, axis=-1, keepdims=True)
    return xc * lax.rsqrt(var + LN_EPS) * g + b


def _silu(x):
    return x * jax.nn.sigmoid(x)


def _pack_bf16_pairs(x):
    c = x.shape[1] // 2
    bits = lax.bitcast_convert_type(x.astype(BF16).astype(F32), U32)
    return (bits[:, :c] >> 16) | (bits[:, c:] & jnp.uint32(0xFFFF0000))


def _unpack_bf16_pairs(w):
    lo = lax.bitcast_convert_type(w << 16, F32)
    hi = lax.bitcast_convert_type(w & jnp.uint32(0xFFFF0000), F32)
    return lo, hi


def _rope128(c, ra, rb, rc):
    return c * ra + pltpu.roll(c, LANES - 8, 1) * rb + pltpu.roll(c, 8, 1) * rc


def _head_variants(x2):
    lo = lax.broadcasted_iota(I32, (1, LANES), 1) < HEAD_DIM
    sw = pltpu.roll(x2, HEAD_DIM, 1)
    zero = jnp.zeros_like(x2)
    return (jnp.where(lo, x2, zero).astype(BF16), jnp.where(lo, zero, sw).astype(BF16),
            jnp.where(lo, sw, zero).astype(BF16), jnp.where(lo, zero, x2).astype(BF16))


def _softmax_sink(s, extra, sink):
    m = jnp.maximum(jnp.max(s, axis=1, keepdims=True), sink)
    if extra is not None:
        m = jnp.maximum(m, extra)
    p = jnp.exp(s - m)
    den = jnp.sum(p, axis=1, keepdims=True) + jnp.exp(sink - m)
    if extra is None:
        return p * (1.0 / den), None
    pe = jnp.exp(extra - m)
    r = 1.0 / (den + pe)
    return p * r, pe * r


def _ada_kernel(c_ref, w_ref, b_ref, o_ref):
    s = _silu(c_ref[...]).astype(BF16)
    o_ref[...] = jnp.dot(s, w_ref[...].astype(BF16), preferred_element_type=F32) + b_ref[...]


def _ada(c_all, w_ada, b_ada):
    rows, n = c_all.shape[0], w_ada.shape[1]
    tn = 1536
    return pl.pallas_call(
        _ada_kernel,
        out_shape=jax.ShapeDtypeStruct((rows, n), F32),
        grid=(n // tn,),
        in_specs=[pl.BlockSpec((rows, D_MODEL), lambda j: (0, 0)),
                  pl.BlockSpec((D_MODEL, tn), lambda j: (0, j)),
                  pl.BlockSpec((1, tn), lambda j: (0, j))],
        out_specs=pl.BlockSpec((rows, tn), lambda j: (0, j)),
        compiler_params=_cparams(),
        name="ada",
    )(c_all, w_ada, b_ada.reshape(1, n))


def _mixer_prompt_kernel(sink_ref, x_ref, mods_ref, ra_ref, rb_ref, rc_ref, win_ref, wpool_ref,
                         pscale_ref, wout_ref, g1_ref, b1_ref,
                         x1_ref, h2_ref, kwin_ref, vwin_ref, pst_ref,
                         q_s, kk_s, vv_s, u_s, attn_s, pool_s):
    i = pl.program_id(0)
    last = pl.num_programs(0) - 1

    @pl.when(i == 0)
    def _():
        kk_s[:, :, 0:WINDOW, :] = jnp.zeros((N_KV_HEADS, 2, WINDOW, LANES), BF16)
        vv_s[:, :, 0:WINDOW, :] = jnp.zeros((N_KV_HEADS, 2, WINDOW, LANES), BF16)
        u_s[0:HALO_ROWS, :] = jnp.zeros((HALO_ROWS, D_MODEL), F32)

    shift1 = mods_ref[:, 0:1024]
    scale1 = mods_ref[:, 1024:2048]
    gate1 = mods_ref[:, 2048:3072]
    shift2 = mods_ref[:, 3072:4096]
    scale2 = mods_ref[:, 4096:5120]

    x = x_ref[...]
    h = (x * (1.0 + scale1) + shift1).astype(BF16)
    ra, rb, rc = ra_ref[...], rb_ref[...], rc_ref[...]

    q = jnp.dot(h, win_ref[:, Q0:K0], preferred_element_type=F32)
    for c in range(D_MODEL // LANES):
        qc = _rope128(q[:, c * LANES:(c + 1) * LANES], ra, rb, rc)
        q_s[:, c * LANES:(c + 1) * LANES] = (qc * (HEAD_DIM ** -0.5)).astype(BF16)

    k = jnp.dot(h, win_ref[:, K0:V0], preferred_element_type=F32)
    v = jnp.dot(h, win_ref[:, V0:U0], preferred_element_type=F32)
    for c in range(KV_WIDTH // LANES):
        kr = _rope128(k[:, c * LANES:(c + 1) * LANES], ra, rb, rc)
        vr = v[:, c * LANES:(c + 1) * LANES]

        @pl.when(i == last)
        def _(kr=kr, vr=vr, c=c):
            kwin_ref[:, c * LANES:(c + 1) * LANES] = kr[TQ - WINDOW:, :]
            vwin_ref[:, c * LANES:(c + 1) * LANES] = vr[TQ - WINDOW:, :]

        ka, kb, kc_, kd = _head_variants(kr)
        va, vb, vc_, vd = _head_variants(vr)
        kk_s[2 * c, 0, WINDOW:, :] = ka
        kk_s[2 * c, 1, WINDOW:, :] = kb
        kk_s[2 * c + 1, 0, WINDOW:, :] = kc_
        kk_s[2 * c + 1, 1, WINDOW:, :] = kd
        vv_s[2 * c, 0, WINDOW:, :] = va
        vv_s[2 * c, 1, WINDOW:, :] = vb
        vv_s[2 * c + 1, 0, WINDOW:, :] = vc_
        vv_s[2 * c + 1, 1, WINDOW:, :] = vd

    u_s[HALO_ROWS:, :] = jnp.dot(h, win_ref[:, U0:GA0], preferred_element_type=F32)

    @pl.when(i == last)
    def _():
        pst_ref[...] = u_s[HALO_ROWS + TQ - POOL_HALO:, :]

    rowpos = i * TQ + lax.broadcasted_iota(I32, (TQ, 1), 0)
    for g, w in enumerate(POOL_WINDOWS):
        cs = slice(g * POOL_GROUP_DIM, (g + 1) * POOL_GROUP_DIM)
        acc = u_s[HALO_ROWS:, cs]
        for m in range(1, w):
            acc = acc + u_s[HALO_ROWS - m:HALO_ROWS - m + TQ, cs]
        cnt = jnp.minimum(w, rowpos + 1).astype(F32)
        d = acc / cnt - u_s[HALO_ROWS:, cs]
        y = jnp.dot(d.astype(BF16), wpool_ref[g], preferred_element_type=F32)
        pool_s[:, cs] = y * pscale_ref[:, cs]

    a_idx = lax.broadcasted_iota(I32, (WINDOW, 4 * WINDOW), 0)
    j_idx = lax.broadcasted_iota(I32, (WINDOW, 4 * WINDOW), 1) & (2 * WINDOW - 1)
    no_prev = jnp.where(i > 0, 0, 4 * WINDOW)
    for qb in range(TQ // WINDOW):
        rows = slice(qb * WINDOW, (qb + 1) * WINDOW)
        band = slice(qb * WINDOW, (qb + 2) * WINDOW)
        a_prev = a_idx + no_prev if qb == 0 else a_idx
        mask = jnp.logical_or(jnp.logical_and(j_idx < WINDOW, j_idx >= a_prev),
                              jnp.logical_and(j_idx >= WINDOW, (j_idx - WINDOW) <= a_idx))
        for g in range(N_KV_HEADS):
            k2 = jnp.concatenate([kk_s[g, 0, band, :], kk_s[g, 1, band, :]], axis=0)
            v2 = jnp.concatenate([vv_s[g, 0, band, :], vv_s[g, 1, band, :]], axis=0)
            for p in range(2):
                c = 2 * g + p
                q2 = q_s[rows, c * LANES:(c + 1) * LANES]
                s2 = lax.dot_general(q2, k2, (((1,), (1,)), ((), ())), preferred_element_type=F32)
                s2 = jnp.where(mask, s2, NEG_INF)
                pa, _ = _softmax_sink(s2[:, :2 * WINDOW], None, sink_ref[2 * c])
                pb, _ = _softmax_sink(s2[:, 2 * WINDOW:], None, sink_ref[2 * c + 1])
                p2 = jnp.concatenate([pa, pb], axis=1).astype(BF16)
                attn_s[rows, c * LANES:(c + 1) * LANES] = jnp.dot(p2, v2, preferred_element_type=F32)

    kk_s[:, :, 0:WINDOW, :] = kk_s[:, :, TQ:, :]
    vv_s[:, :, 0:WINDOW, :] = vv_s[:, :, TQ:, :]
    u_s[0:HALO_ROWS, :] = u_s[TQ:, :]

    ga = jnp.dot(h, win_ref[:, GA0:GB0], preferred_element_type=F32)
    gb = jnp.dot(h, win_ref[:, GB0:IN_WIDTH], preferred_element_type=F32)
    merged = jax.nn.sigmoid(ga) * attn_s[...] + jax.nn.sigmoid(gb) * pool_s[...]
    y = jnp.dot(merged.astype(BF16), wout_ref[...], preferred_element_type=F32)
    x1 = _layer_norm(ALPHA * x + gate1 * y, g1_ref[...], b1_ref[...])
    x1_ref[...] = x1
    h2_ref[...] = _pack_bf16_pairs(x1 * (1.0 + scale2) + shift2)


def _mixer_prompt(x, mods_p, tabs, w_in_b, sinks, w_pool_b, pool_scale, w_out_b, ln1_g, ln1_b):
    t = x.shape[0]
    full = lambda shape: pl.BlockSpec(shape, lambda i: (0,) * len(shape))
    rowblk = lambda width: pl.BlockSpec((TQ, width), lambda i: (i, 0))
    return pl.pallas_call(
        _mixer_prompt_kernel,
        out_shape=(jax.ShapeDtypeStruct((t, D_MODEL), F32),
                   jax.ShapeDtypeStruct((t, D_MODEL // 2), U32),
                   jax.ShapeDtypeStruct((WINDOW, KV_WIDTH), F32),
                   jax.ShapeDtypeStruct((WINDOW, KV_WIDTH), F32),
                   jax.ShapeDtypeStruct((POOL_HALO, D_MODEL), F32)),
        grid=(t // TQ,),
        in_specs=[pl.BlockSpec(memory_space=pltpu.SMEM),
                  rowblk(D_MODEL), full((1, 6 * D_MODEL)),
                  rowblk(LANES), rowblk(LANES), rowblk(LANES),
                  full((D_MODEL, IN_WIDTH)), full((4, POOL_GROUP_DIM, POOL_GROUP_DIM)),
                  full((1, D_MODEL)), full((D_MODEL, D_MODEL)), full((1, D_MODEL)), full((1, D_MODEL))],
        out_specs=(rowblk(D_MODEL), rowblk(D_MODEL // 2),
                   full((WINDOW, KV_WIDTH)), full((WINDOW, KV_WIDTH)), full((POOL_HALO, D_MODEL))),
        scratch_shapes=[pltpu.VMEM((TQ, D_MODEL), BF16),
                        pltpu.VMEM((N_KV_HEADS, 2, WINDOW + TQ, LANES), BF16),
                        pltpu.VMEM((N_KV_HEADS, 2, WINDOW + TQ, LANES), BF16),
                        pltpu.VMEM((HALO_ROWS + TQ, D_MODEL), F32),
                        pltpu.VMEM((TQ, D_MODEL), F32),
                        pltpu.VMEM((TQ, D_MODEL), F32)],
        compiler_params=_cparams(),
        name="mixer_prompt",
    )(sinks, x, mods_p, *tabs, w_in_b, w_pool_b, pool_scale, w_out_b, ln1_g, ln1_b)


def _sample_proj_kernel(x_ref, mods_ref, ra_ref, rb_ref, rc_ref, win_ref, spt_ref, wpool_ref, pscale_ref,
                        q_ref, k_ref, v_ref, u_ref, pool_ref, ga_ref, gb_ref):
    x = x_ref[...]
    h = (x * (1.0 + mods_ref[:, 1024:2048]) + mods_ref[:, 0:1024]).astype(BF16)
    ra, rb, rc = ra_ref[...], rb_ref[...], rc_ref[...]
    q = jnp.dot(h, win_ref[:, Q0:K0], preferred_element_type=F32)
    for c in range(D_MODEL // LANES):
        qc = _rope128(q[:, c * LANES:(c + 1) * LANES], ra, rb, rc)
        q_ref[:, c * LANES:(c + 1) * LANES] = (qc * (HEAD_DIM ** -0.5)).astype(BF16)
    k = jnp.dot(h, win_ref[:, K0:V0], preferred_element_type=F32)
    for c in range(KV_WIDTH // LANES):
        k_ref[:, c * LANES:(c + 1) * LANES] = _rope128(k[:, c * LANES:(c + 1) * LANES], ra, rb, rc)
    v_ref[...] = jnp.dot(h, win_ref[:, V0:U0], preferred_element_type=F32)
    u = jnp.dot(h, win_ref[:, U0:GA0], preferred_element_type=F32)
    u_ref[...] = u
    for g, w in enumerate(POOL_WINDOWS):
        cs = slice(g * POOL_GROUP_DIM, (g + 1) * POOL_GROUP_DIM)
        acc = u[:, cs]
        for m in range(1, w):
            acc = acc + spt_ref[POOL_HALO - m, :, cs]
        d = acc / float(min(w, PAST_LEN + 1)) - u[:, cs]
        y = jnp.dot(d.astype(BF16), wpool_ref[g], preferred_element_type=F32)
        pool_ref[:, cs] = y * pscale_ref[:, cs]
    ga_ref[...] = jnp.dot(h, win_ref[:, GA0:GB0], preferred_element_type=F32)
    gb_ref[...] = jnp.dot(h, win_ref[:, GB0:IN_WIDTH], preferred_element_type=F32)


def _sample_proj(x_s, mods_s, tabs, w_in_b, sp_t, w_pool_b, pool_scale):
    b = x_s.shape[0]
    sd = lambda w, dt=F32: jax.ShapeDtypeStruct((b, w), dt)
    return pl.pallas_call(
        _sample_proj_kernel,
        out_shape=(sd(D_MODEL, BF16), sd(KV_WIDTH), sd(KV_WIDTH), sd(D_MODEL), sd(D_MODEL),
                   sd(D_MODEL), sd(D_MODEL)),
        compiler_params=pltpu.CompilerParams(vmem_limit_bytes=VMEM_LIMIT_BYTES),
        name="sample_proj",
    )(x_s, mods_s, *tabs, w_in_b, sp_t, w_pool_b, pool_scale)


def _sample_attn_kernel(sink_ref, q_ref, ck_ref, cv_ref, kn_ref, vn_ref, o_ref):
    nkeys = SB * WINDOW
    lo = lax.broadcasted_iota(I32, (1, LANES), 1) < HEAD_DIM
    row_b = lax.broadcasted_iota(I32, (2 * SB, 2 * nkeys), 0) & (SB - 1)
    col_b = (lax.broadcasted_iota(I32, (2 * SB, 2 * nkeys), 1) & (nkeys - 1)) >> 7
    own = row_b == col_b
    first_pair = lax.broadcasted_iota(I32, (2 * SB, 1), 0) < SB
    for c in range(KV_WIDTH // LANES):
        kvar = _head_variants(ck_ref[:, c * LANES:(c + 1) * LANES])
        vvar = _head_variants(cv_ref[:, c * LANES:(c + 1) * LANES])
        kn = kn_ref[:, c * LANES:(c + 1) * LANES].astype(BF16).astype(F32)
        vn = vn_ref[:, c * LANES:(c + 1) * LANES].astype(BF16).astype(F32)
        kn_sw, vn_sw = pltpu.roll(kn, HEAD_DIM, 1), pltpu.roll(vn, HEAD_DIM, 1)
        for e in range(2):
            g = 2 * c + e
            k2 = jnp.concatenate([kvar[2 * e], kvar[2 * e + 1]], axis=0)
            v2 = jnp.concatenate([vvar[2 * e], vvar[2 * e + 1]], axis=0)
            kdup = jnp.where(lo, kn, kn_sw) if e == 0 else jnp.where(lo, kn_sw, kn)
            vdup = jnp.where(lo, vn, vn_sw) if e == 0 else jnp.where(lo, vn_sw, vn)
            qa = q_ref[:, (2 * g) * LANES:(2 * g + 1) * LANES]
            qb = q_ref[:, (2 * g + 1) * LANES:(2 * g + 2) * LANES]
            q4 = jnp.concatenate([qa, qb], axis=0)
            s = lax.dot_general(q4, k2, (((1,), (1,)), ((), ())), preferred_element_type=F32)
            s = jnp.where(own, s, NEG_INF)
            prod = q4.astype(F32) * jnp.concatenate([kdup, kdup], axis=0)
            s_new = (jnp.sum(jnp.where(lo, prod, 0.0), axis=1, keepdims=True),
                     jnp.sum(jnp.where(lo, 0.0, prod), axis=1, keepdims=True))
            ps, pn = [], []
            for hh in range(2):
                sink = jnp.where(first_pair, sink_ref[4 * g + hh], sink_ref[4 * g + 2 + hh])
                p_c, p_n = _softmax_sink(s[:, hh * nkeys:(hh + 1) * nkeys], s_new[hh], sink)
                ps.append(p_c)
                pn.append(p_n)
            p2 = jnp.concatenate(ps, axis=1).astype(BF16)
            o = jnp.dot(p2, v2, preferred_element_type=F32)
            o = o + jnp.where(lo, pn[0], pn[1]) * jnp.concatenate([vdup, vdup], axis=0)
            o_ref[:, (2 * g) * LANES:(2 * g + 1) * LANES] = o[:SB]
            o_ref[:, (2 * g + 1) * LANES:(2 * g + 2) * LANES] = o[SB:]


def _sample_attn(sinks, q, ck2d, cv2d, k_new, v_new):
    b = q.shape[0]
    return pl.pallas_call(
        _sample_attn_kernel,
        out_shape=jax.ShapeDtypeStruct((b, D_MODEL), F32),
        grid=(b // SB,),
        in_specs=[pl.BlockSpec(memory_space=pltpu.SMEM),
                  pl.BlockSpec((SB, D_MODEL), lambda i: (i, 0)),
                  pl.BlockSpec((SB * WINDOW, KV_WIDTH), lambda i: (i, 0)),
                  pl.BlockSpec((SB * WINDOW, KV_WIDTH), lambda i: (i, 0)),
                  pl.BlockSpec((SB, KV_WIDTH), lambda i: (i, 0)),
                  pl.BlockSpec((SB, KV_WIDTH), lambda i: (i, 0))],
        out_specs=pl.BlockSpec((SB, D_MODEL), lambda i: (i, 0)),
        compiler_params=_cparams(),
        name="sample_attn",
    )(sinks, q, ck2d, cv2d, k_new, v_new)


def _sample_merge_kernel(x_ref, mods_ref, attn_ref, pool_ref, ga_ref, gb_ref, wout_ref, g1_ref, b1_ref,
                         x1_ref, h2_ref):
    merged = jax.nn.sigmoid(ga_ref[...]) * attn_ref[...] + jax.nn.sigmoid(gb_ref[...]) * pool_ref[...]
    y = jnp.dot(merged.astype(BF16), wout_ref[...], preferred_element_type=F32)
    x1 = _layer_norm(ALPHA * x_ref[...] + mods_ref[:, 2048:3072] * y, g1_ref[...], b1_ref[...])
    x1_ref[...] = x1
    h2_ref[...] = _pack_bf16_pairs(x1 * (1.0 + mods_ref[:, 4096:5120]) + mods_ref[:, 3072:4096])


def _sample_merge(x_s, mods_s, attn, pool, ga, gb, w_out_b, ln1_g, ln1_b):
    b = x_s.shape[0]
    return pl.pallas_call(
        _sample_merge_kernel,
        out_shape=(jax.ShapeDtypeStruct((b, D_MODEL), F32), jax.ShapeDtypeStruct((b, D_MODEL // 2), U32)),
        compiler_params=pltpu.CompilerParams(vmem_limit_bytes=VMEM_LIMIT_BYTES),
        name="sample_merge",
    )(x_s, mods_s, attn, pool, ga, gb, w_out_b, ln1_g, ln1_b)


def _router_kernel(h2_ref, wrt_ref, bias_ref, idx_ref, wts_ref, rank_ref, cnt_ref, carry_s):
    i = pl.program_id(0)

    @pl.when(i == 0)
    def _():
        carry_s[...] = jnp.zeros_like(carry_s)

    lo, hi = _unpack_bf16_pairs(h2_ref[...])
    nt = (((1,), (1,)), ((), ()))
    half = D_MODEL // 2
    logits = (lax.dot_general(wrt_ref[:, :half], lo.astype(BF16), nt, preferred_element_type=F32)
              + lax.dot_general(wrt_ref[:, half:], hi.astype(BF16), nt, preferred_element_type=F32))
    scores = jax.nn.sigmoid(logits)
    sel = scores + bias_ref[...]

    gi = lax.broadcasted_iota(I32, (EXPERTS_PER_GROUP, TB), 0).astype(F32)
    gscore = []
    for g in range(N_EXPERT_GROUPS):
        blk = sel[g * EXPERTS_PER_GROUP:(g + 1) * EXPERTS_PER_GROUP, :]
        m1 = jnp.max(blk, axis=0, keepdims=True)
        f1 = jnp.min(jnp.where(blk == m1, gi, float(EXPERTS_PER_GROUP)), axis=0, keepdims=True)
        m2 = jnp.max(jnp.where(gi == f1, NEG_INF, blk), axis=0, keepdims=True)
        gscore.append(m1 + m2)

    gsel = [jnp.zeros((1, TB), jnp.bool_) for _ in range(N_EXPERT_GROUPS)]
    for _ in range(TOPK_GROUPS):
        m = gscore[0]
        for g in range(1, N_EXPERT_GROUPS):
            m = jnp.maximum(m, gscore[g])
        found = jnp.zeros((1, TB), jnp.bool_)
        for g in range(N_EXPERT_GROUPS):
            pick = jnp.logical_and(gscore[g] == m, jnp.logical_not(found))
            found = jnp.logical_or(found, pick)
            gsel[g] = jnp.logical_or(gsel[g], pick)
            gscore[g] = jnp.where(pick, NEG_INF, gscore[g])
    cur = jnp.concatenate(
        [jnp.where(gsel[g], sel[g * EXPERTS_PER_GROUP:(g + 1) * EXPERTS_PER_GROUP, :], NEG_INF)
         for g in range(N_EXPERT_GROUPS)], axis=0)

    ei = lax.broadcasted_iota(I32, (N_EXPERTS, TB), 0).astype(F32)
    onehot = jnp.zeros((N_EXPERTS, TB), F32)
    picks, raw_w = [], []
    for _ in range(TOP_K):
        m = jnp.max(cur, axis=0, keepdims=True)
        f = jnp.min(jnp.where(cur == m, ei, float(N_EXPERTS)), axis=0, keepdims=True)
        pick = ei == f
        picks.append(f)
        raw_w.append(jnp.sum(jnp.where(pick, scores, 0.0), axis=0, keepdims=True))
        cur = jnp.where(pick, NEG_INF, cur)
        onehot = onehot + pick.astype(F32)
    wsum = raw_w[0]
    for k in range(1, TOP_K):
        wsum = wsum + raw_w[k]

    tri = (lax.broadcasted_iota(I32, (TB, TB), 0) <= lax.broadcasted_iota(I32, (TB, TB), 1)).astype(BF16)
    incl = jnp.dot(onehot.astype(BF16), tri, preferred_element_type=F32)
    pos = carry_s[...] + incl - onehot
    for k in range(TOP_K):
        idx_ref[k:k + 1, :] = picks[k].astype(I32)
        wts_ref[k:k + 1, :] = raw_w[k] / wsum * ROUTED_SCALE
        rank_ref[k:k + 1, :] = jnp.sum(jnp.where(ei == picks[k], pos, 0.0), axis=0, keepdims=True).astype(I32)
    carry = carry_s[...] + jnp.sum(onehot, axis=1, keepdims=True)
    carry_s[...] = carry
    cnt_ref[...] = carry


def _router(h2_all, w_router_t_b, router_bias):
    t_all = h2_all.shape[0]
    tok = lambda dt: jax.ShapeDtypeStruct((TOP_K, t_all), dt)
    return pl.pallas_call(
        _router_kernel,
        out_shape=(tok(I32), tok(F32), tok(I32), jax.ShapeDtypeStruct((N_EXPERTS, LANES), F32)),
        grid=(t_all // TB,),
        in_specs=[pl.BlockSpec((TB, D_MODEL // 2), lambda i: (i, 0)),
                  pl.BlockSpec((N_EXPERTS, D_MODEL), lambda i: (0, 0)),
                  pl.BlockSpec((N_EXPERTS, 1), lambda i: (0, 0))],
        out_specs=(pl.BlockSpec((TOP_K, TB), lambda i: (0, i)),
                   pl.BlockSpec((TOP_K, TB), lambda i: (0, i)),
                   pl.BlockSpec((TOP_K, TB), lambda i: (0, i)),
                   pl.BlockSpec((N_EXPERTS, LANES), lambda i: (0, 0))),
        scratch_shapes=[pltpu.VMEM((N_EXPERTS, LANES), F32)],
        compiler_params=_cparams(),
        name="router",
    )(h2_all, w_router_t_b, router_bias.reshape(N_EXPERTS, 1))


def _moe_meta_kernel(idx_ref, rank_ref, cnt_ref, dest_ref, emeta_ref, tstart_s):
    i = pl.program_id(0)

    @pl.when(i == 0)
    def _():
        cnt_col = cnt_ref[:, 0:1]
        tiles_col = jnp.floor((cnt_col + float(ROW_TILE - 1)) * (1.0 / ROW_TILE))
        e_r = lax.broadcasted_iota(I32, (N_EXPERTS, N_EXPERTS), 0)
        e_c = lax.broadcasted_iota(I32, (N_EXPERTS, N_EXPERTS), 1)
        diag = e_r == e_c
        tiles_row = jnp.sum(jnp.where(diag, tiles_col, 0.0), axis=0, keepdims=True)
        cnt_row = jnp.sum(jnp.where(diag, cnt_col, 0.0), axis=0, keepdims=True)
        tstart_row = jnp.sum(jnp.where(e_r < e_c, tiles_col, 0.0), axis=0, keepdims=True)
        tstart_col = jnp.sum(jnp.where(e_c < e_r, tiles_row, 0.0), axis=1, keepdims=True)
        tstart_s[...] = jnp.broadcast_to(tstart_col, (N_EXPERTS, LANES))
        emeta_ref[0:1, :] = tstart_row.astype(I32)
        emeta_ref[1:2, :] = tiles_row.astype(I32)
        emeta_ref[2:3, :] = cnt_row.astype(I32)
        emeta_ref[3:8, :] = jnp.zeros((5, N_EXPERTS), I32)

    ei = lax.broadcasted_iota(I32, (N_EXPERTS, TB), 0)
    tstart = tstart_s[...]
    for k in range(TOP_K):
        st = jnp.sum(jnp.where(ei == idx_ref[k:k + 1, :], tstart, 0.0), axis=0, keepdims=True)
        dest_ref[k:k + 1, :] = st.astype(I32) * ROW_TILE + rank_ref[k:k + 1, :]


def _moe_meta(idx, rank, counts):
    t_all = idx.shape[1]
    return pl.pallas_call(
        _moe_meta_kernel,
        out_shape=(jax.ShapeDtypeStruct((TOP_K, t_all), I32), jax.ShapeDtypeStruct((8, N_EXPERTS), I32)),
        grid=(t_all // TB,),
        in_specs=[pl.BlockSpec((TOP_K, TB), lambda i: (0, i)),
                  pl.BlockSpec((TOP_K, TB), lambda i: (0, i)),
                  pl.BlockSpec((N_EXPERTS, LANES), lambda i: (0, 0))],
        out_specs=(pl.BlockSpec((TOP_K, TB), lambda i: (0, i)),
                   pl.BlockSpec((8, N_EXPERTS), lambda i: (0, 0))),
        scratch_shapes=[pltpu.VMEM((N_EXPERTS, LANES), F32)],
        compiler_params=_cparams(),
        name="moe_meta",
    )(idx, rank, counts)


def _zero_tail(zbuf, out_ref, zsem, base):
    zbuf[...] = jnp.zeros_like(zbuf)
    copies = [pltpu.make_async_copy(zbuf, out_ref.at[pl.ds(base + j * ZROWS, ZROWS), :], zsem)
              for j in range(TAIL_ROWS // ZROWS)]
    for c in copies:
        c.start()
    for c in copies:
        c.wait()


def _dispatch_kernel(emeta_ref, dest_ref, h2_ref, xs_ref, zbuf, sem, zsem, *, base):
    i = pl.program_id(0)

    @pl.when(i == 0)
    def _():
        _zero_tail(zbuf, xs_ref, zsem, base)

        def gap_rows(e, fn):
            cnt = emeta_ref[2 * N_EXPERTS + e]
            first = emeta_ref[e] * ROW_TILE + cnt
            n_gap = emeta_ref[N_EXPERTS + e] * ROW_TILE - cnt

            def body(r, carry):
                fn(pltpu.make_async_copy(zbuf.at[pl.ds(0, 1), :], xs_ref.at[pl.ds(first + r, 1), :], zsem))
                return carry

            lax.fori_loop(0, n_gap, body, 0)

        def start_gaps(e, carry):
            gap_rows(e, lambda c: c.start())
            return carry

        def wait_gaps(e, carry):
            gap_rows(e, lambda c: c.wait())
            return carry

        lax.fori_loop(0, N_EXPERTS, start_gaps, 0)
        lax.fori_loop(0, N_EXPERTS, wait_gaps, 0)

    def row_copy(t, d):
        return pltpu.make_async_copy(h2_ref.at[pl.ds(t, 1), :], xs_ref.at[pl.ds(d, 1), :], sem)

    def start(t, carry):
        for k in range(TOP_K):
            row_copy(t, dest_ref[0, 0, k * TB + t]).start(priority=k % 2)
        return carry

    lax.fori_loop(0, TB, start, 0)

    def wait(t, carry):
        for k in range(TOP_K):
            row_copy(t, dest_ref[0, 0, k * TB + t]).wait()
        return carry

    lax.fori_loop(0, TB, wait, 0)


def _dispatch(emeta_flat, dest_blk, h2_all):
    t_all = h2_all.shape[0]
    base = t_all * TOP_K
    grid_spec = pltpu.PrefetchScalarGridSpec(
        num_scalar_prefetch=1,
        grid=(t_all // TB,),
        in_specs=[pl.BlockSpec((1, 1, TOP_K * TB), lambda i, m: (i, 0, 0), memory_space=pltpu.SMEM),
                  pl.BlockSpec((TB, D_MODEL // 2), lambda i, m: (i, 0))],
        out_specs=pl.BlockSpec(memory_space=pl.ANY),
        scratch_shapes=[pltpu.VMEM((ZROWS, D_MODEL // 2), U32),
                        pltpu.SemaphoreType.DMA(()), pltpu.SemaphoreType.DMA(())],
    )
    return pl.pallas_call(
        functools.partial(_dispatch_kernel, base=base),
        out_shape=jax.ShapeDtypeStruct((base + TAIL_ROWS, D_MODEL // 2), U32),
        grid_spec=grid_spec,
        compiler_params=_cparams(),
        name="dispatch",
    )(emeta_flat, dest_blk, h2_all)


def _experts_kernel(emeta_ref, xs_ref, wg_ref, wu_ref, wd_ref, ys_ref,
                    xbuf, ybuf, zbuf, wgu_s, wd_s, isem, osem, zsem, *, base):
    e = pl.program_id(0)
    row0 = emeta_ref[e] * ROW_TILE
    n_tiles = emeta_ref[N_EXPERTS + e]
    n_chunks = (n_tiles + (CH_TILES - 1)) >> CH_SHIFT

    def load(row, slot):
        return pltpu.make_async_copy(xs_ref.at[pl.ds(pl.multiple_of(row, ROW_TILE), CH), :],
                                     xbuf.at[slot], isem.at[slot])

    def store_tile(row, t):
        return pltpu.make_async_copy(
            ybuf.at[pl.ds(pl.multiple_of(t * ROW_TILE, ROW_TILE), ROW_TILE), :],
            ys_ref.at[pl.ds(pl.multiple_of(row + t * ROW_TILE, ROW_TILE), ROW_TILE), :], osem)

    def for_tiles(j, fn):
        n = jnp.where(j >= 0, jnp.clip(n_tiles - j * CH_TILES, 0, CH_TILES), 0)

        def body(t, carry):
            fn(store_tile(row0 + j * CH, t))
            return carry

        lax.fori_loop(0, n, body, 0)

    @pl.when(e == 0)
    def _():
        _zero_tail(zbuf, ys_ref, zsem, base)

        @pl.when(n_tiles > 0)
        def _():
            load(row0, 0).start()

    wgu_s[:, :EXPERT_DIM] = wg_ref[...].astype(BF16)
    wgu_s[:, EXPERT_DIM:] = wu_ref[...].astype(BF16)
    wd_s[...] = wd_ref[...].astype(BF16)

    def chunk(j, carry):
        slot = j & 1
        load(row0 + j * CH, slot).wait()

        @pl.when(j + 1 < n_chunks)
        def _():
            load(row0 + (j + 1) * CH, 1 - slot).start()

        lo, hi = _unpack_bf16_pairs(xbuf[slot])
        half = D_MODEL // 2
        gu = (jnp.dot(lo.astype(BF16), wgu_s[:half, :], preferred_element_type=F32)
              + jnp.dot(hi.astype(BF16), wgu_s[half:, :], preferred_element_type=F32))
        act = _silu(gu[:, :EXPERT_DIM]) * gu[:, EXPERT_DIM:]
        y = _pack_bf16_pairs(jnp.dot(act.astype(BF16), wd_s[...], preferred_element_type=F32))
        for_tiles(j - 1, lambda c: c.wait())
        ybuf[...] = y
        for_tiles(j, lambda c: c.start())
        return carry

    lax.fori_loop(0, n_chunks, chunk, 0)
    for_tiles(n_chunks - 1, lambda c: c.wait())

    @pl.when(e + 1 < pl.num_programs(0))
    def _():
        @pl.when(emeta_ref[N_EXPERTS + e + 1] > 0)
        def _():
            load(emeta_ref[e + 1] * ROW_TILE, 0).start()


def _experts(emeta_flat, xs, w_gate, w_up, w_down):
    base = xs.shape[0] - TAIL_ROWS
    w_map = lambda e, m: (e, 0, 0)
    grid_spec = pltpu.PrefetchScalarGridSpec(
        num_scalar_prefetch=1,
        grid=(N_EXPERTS,),
        in_specs=[pl.BlockSpec(memory_space=pl.ANY),
                  pl.BlockSpec((None, D_MODEL, EXPERT_DIM), w_map),
                  pl.BlockSpec((None, D_MODEL, EXPERT_DIM), w_map),
                  pl.BlockSpec((None, EXPERT_DIM, D_MODEL), w_map)],
        out_specs=pl.BlockSpec(memory_space=pl.ANY),
        scratch_shapes=[pltpu.VMEM((2, CH, D_MODEL // 2), U32),
                        pltpu.VMEM((CH, D_MODEL // 2), U32),
                        pltpu.VMEM((ZROWS, D_MODEL // 2), U32),
                        pltpu.VMEM((D_MODEL, 2 * EXPERT_DIM), BF16),
                        pltpu.VMEM((EXPERT_DIM, D_MODEL), BF16),
                        pltpu.SemaphoreType.DMA((2,)), pltpu.SemaphoreType.DMA(()),
                        pltpu.SemaphoreType.DMA(())],
    )
    return pl.pallas_call(
        functools.partial(_experts_kernel, base=base),
        out_shape=jax.ShapeDtypeStruct(xs.shape, U32),
        grid_spec=grid_spec,
        compiler_params=_cparams(),
        name="experts",
    )(emeta_flat, xs, w_gate, w_up, w_down)


def _combine_kernel(dest_ref, dnext_ref, x1_ref, mods_ref, wts_ref, ys_ref, wsgu_ref, wsd_ref, g2_ref, b2_ref,
                    out_ref, buf, sem):
    i = pl.program_id(0)
    slot = i & 1

    def row_copy(dref, s, t, k):
        return pltpu.make_async_copy(ys_ref.at[pl.ds(dref[0, 0, k * TB + t], 1), :],
                                     buf.at[s, k, pl.ds(t, 1), :], sem.at[s])

    def gather(dref, s):
        def start(t, carry):
            for k in range(TOP_K):
                row_copy(dref, s, t, k).start(priority=k % 2)
            return carry

        lax.fori_loop(0, TB, start, 0)

    @pl.when(i == 0)
    def _():
        gather(dest_ref, 0)

    @pl.when(i + 1 < pl.num_programs(0))
    def _():
        gather(dnext_ref, 1 - slot)

    x1 = x1_ref[...]
    h2 = (x1 * (1.0 + mods_ref[:, 4096:5120]) + mods_ref[:, 3072:4096]).astype(BF16)
    gu = jnp.dot(h2, wsgu_ref[...], preferred_element_type=F32)
    act = _silu(gu[:, :SHARED_DIM]) * gu[:, SHARED_DIM:]
    shared = jnp.dot(act.astype(BF16), wsd_ref[...], preferred_element_type=F32)

    def wait(t, carry):
        for k in range(TOP_K):
            row_copy(dest_ref, slot, t, k).wait()
        return carry

    lax.fori_loop(0, TB, wait, 0)

    acc_lo = jnp.zeros((TB, D_MODEL // 2), F32)
    acc_hi = jnp.zeros((TB, D_MODEL // 2), F32)
    for k in range(TOP_K):
        lo, hi = _unpack_bf16_pairs(buf[slot, k])
        wk = wts_ref[:, k:k + 1]
        acc_lo = acc_lo + wk * lo
        acc_hi = acc_hi + wk * hi
    m = shared + jnp.concatenate([acc_lo, acc_hi], axis=1)
    out_ref[...] = _layer_norm(ALPHA * x1 + mods_ref[:, 5120:6144] * m, g2_ref[...], b2_ref[...])


def _combine(dest_blk, x1, mods, wts_t, ys, wsgu_b, wsd_b, ln2_g, ln2_b, blk_off):
    t = x1.shape[0]
    per_row = mods.shape[0] != 1
    mods_spec = (pl.BlockSpec((TB, 6 * D_MODEL), lambda i: (i, 0)) if per_row
                 else pl.BlockSpec((1, 6 * D_MODEL), lambda i: (0, 0)))
    full = lambda shape: pl.BlockSpec(shape, lambda i: (0,) * len(shape))
    n_steps = t // TB
    return pl.pallas_call(
        _combine_kernel,
        out_shape=jax.ShapeDtypeStruct((t, D_MODEL), F32),
        grid=(n_steps,),
        in_specs=[pl.BlockSpec((1, 1, TOP_K * TB), lambda i: (i + blk_off, 0, 0), memory_space=pltpu.SMEM),
                  pl.BlockSpec((1, 1, TOP_K * TB), lambda i: (jnp.minimum(i + 1, n_steps - 1) + blk_off, 0, 0),
                               memory_space=pltpu.SMEM),
                  pl.BlockSpec((TB, D_MODEL), lambda i: (i, 0)),
                  mods_spec,
                  pl.BlockSpec((TB, TOP_K), lambda i: (i + blk_off, 0)),
                  pl.BlockSpec(memory_space=pl.ANY),
                  full((D_MODEL, 2 * SHARED_DIM)), full((SHARED_DIM, D_MODEL)),
                  full((1, D_MODEL)), full((1, D_MODEL))],
        out_specs=pl.BlockSpec((TB, D_MODEL), lambda i: (i, 0)),
        scratch_shapes=[pltpu.VMEM((2, TOP_K, TB, D_MODEL // 2), U32), pltpu.SemaphoreType.DMA((2,))],
        compiler_params=_cparams(),
        name="combine",
    )(dest_blk, dest_blk, x1, mods, wts_t, ys, wsgu_b, wsd_b, ln2_g, ln2_b)


def _rope_tables(pos):
    half = ROT_DIM // 2
    inv = ROPE_THETA ** (-jnp.arange(0, ROT_DIM, 2, dtype=F32) / ROT_DIM)
    ang = pos.astype(F32)[:, None] * inv[None, :]
    cos, sin = jnp.cos(ang), jnp.sin(ang)
    n = pos.shape[0]
    ones = jnp.ones((n, HEAD_DIM - ROT_DIM), F32)
    z = lambda w: jnp.zeros((n, w), F32)
    ra = jnp.concatenate([cos, cos, ones], axis=1)
    rb = jnp.concatenate([-sin, z(HEAD_DIM - half)], axis=1)
    rc = jnp.concatenate([z(half), sin, z(HEAD_DIM - ROT_DIM)], axis=1)
    return tuple(jnp.tile(t, (1, LANES // HEAD_DIM)) for t in (ra, rb, rc))


def kernel(x_prompt, x_sample, cache_k, cache_v, state_pool, c_prompt, c_sample, w_ada, b_ada, w_in, attn_sinks, w_pool, pool_scale, w_out, ln1_g, ln1_b, w_router, router_bias, w_exp_gate, w_exp_up, w_exp_down, w_sh_gate, w_sh_up, w_sh_down, ln2_g, ln2_b):
    assert w_ada.shape[0] == DEPTH == 1 and x_prompt.shape[0] == 1 and x_sample.shape[1] == 1
    t_p, n_s = x_prompt.shape[1], x_sample.shape[0]
    assert t_p % TQ == 0 and n_s % SB == 0 and n_s == TB and cache_k.shape[2] == WINDOW
    t_all = t_p + n_s

    row = lambda a: a[0].reshape(1, -1)
    w_in_b, w_out_b, w_pool_b = w_in[0].astype(BF16), w_out[0].astype(BF16), w_pool[0].astype(BF16)
    wsgu_b = jnp.concatenate([w_sh_gate[0], w_sh_up[0]], axis=1).astype(BF16)
    wsd_b = w_sh_down[0].astype(BF16)
    w_router_t_b = w_router[0].T.astype(BF16)
    sinks = attn_sinks[0]
    pscale, g1, b1, g2, b2 = row(pool_scale), row(ln1_g), row(ln1_b), row(ln2_g), row(ln2_b)

    x_s = x_sample.reshape(n_s, D_MODEL)
    c_all = jnp.concatenate([c_sample, c_prompt, jnp.zeros((7, D_MODEL), F32)], axis=0)
    mods = _ada(c_all, w_ada[0], b_ada[0])
    mods_s, mods_p = mods[:n_s], mods[n_s:n_s + 1]

    x_p = x_prompt.reshape(t_p, D_MODEL)
    x1_p, h2_p, k_win, v_win, pool_st = _mixer_prompt(
        x_p, mods_p, _rope_tables(jnp.arange(t_p)), w_in_b, sinks, w_pool_b, pscale, w_out_b, g1, b1)

    sp_t = jnp.transpose(state_pool[0], (1, 0, 2))
    q_s, k_new, v_new, u_new, pool_s, ga_s, gb_s = _sample_proj(
        x_s, mods_s, _rope_tables(jnp.full((1,), PAST_LEN)), w_in_b, sp_t, w_pool_b, pscale)
    attn_s = _sample_attn(sinks, q_s, cache_k[0].reshape(n_s * WINDOW, KV_WIDTH),
                          cache_v[0].reshape(n_s * WINDOW, KV_WIDTH), k_new, v_new)
    x1_s, h2_s = _sample_merge(x_s, mods_s, attn_s, pool_s, ga_s, gb_s, w_out_b, g1, b1)

    h2_all = jnp.concatenate([h2_p, h2_s], axis=0)
    idx, wts, rank, counts = _router(h2_all, w_router_t_b, router_bias[0])
    dest, emeta = _moe_meta(idx, rank, counts)
    emeta_flat = emeta[:3].reshape(-1)
    n_tb = t_all // TB
    dest_blk = dest.reshape(TOP_K, n_tb, TB).transpose(1, 0, 2).reshape(n_tb, 1, TOP_K * TB)
    xs = _dispatch(emeta_flat, dest_blk, h2_all)
    ys = _experts(emeta_flat, xs, w_exp_gate[0], w_exp_up[0], w_exp_down[0])
    wts_t = wts.T
    y_p = _combine(dest_blk, x1_p, mods_p, wts_t, ys, wsgu_b, wsd_b, g2, b2, 0)
    y_s = _combine(dest_blk, x1_s, mods_s, wts_t, ys, wsgu_b, wsd_b, g2, b2, t_p // TB)

    k_win_s = jnp.concatenate([cache_k[0][:, 1:], k_new.reshape(n_s, 1, N_KV_HEADS, HEAD_DIM)], axis=1)
    v_win_s = jnp.concatenate([cache_v[0][:, 1:], v_new.reshape(n_s, 1, N_KV_HEADS, HEAD_DIM)], axis=1)
    pool_s_out = jnp.concatenate([state_pool[0][:, 1:], u_new[:, None, :]], axis=1)
    return (y_p.reshape(1, t_p, D_MODEL), y_s.reshape(n_s, 1, D_MODEL),
            k_win.reshape(1, 1, WINDOW, N_KV_HEADS, HEAD_DIM), v_win.reshape(1, 1, WINDOW, N_KV_HEADS, HEAD_DIM),
            pool_st.reshape(1, 1, POOL_HALO, D_MODEL),
            k_win_s[None], v_win_s[None], pool_s_out[None])
```

```python
import functools

import jax
import jax.numpy as jnp
from jax import lax
from jax.experimental import pallas as pl
from jax.experimental.pallas import tpu as pltpu

F32, BF16, I32, U32 = jnp.float32, jnp.bfloat16, jnp.int32, jnp.uint32

D_MODEL = 1024
N_HEADS, HEAD_DIM, N_KV_HEADS = 16, 64, 4
KV_WIDTH = N_KV_HEADS * HEAD_DIM
WINDOW = 128
ROPE_THETA, ROT_DIM = 500000.0, 16
PAST_LEN = 16384
POOL_WINDOWS = (2, 4, 8, 16)
POOL_GROUP_DIM = 256
POOL_HALO = 15
N_EXPERTS, TOP_K, N_EXPERT_GROUPS, TOPK_GROUPS = 256, 8, 8, 4
EXPERTS_PER_GROUP = N_EXPERTS // N_EXPERT_GROUPS
EXPERT_DIM, SHARED_DIM = 256, 256
ROUTED_SCALE = 2.5
LN_EPS = 1e-5
DEPTH = 1
ALPHA = (2.0 * DEPTH) ** 0.25
Q0, K0, V0, U0, GA0, GB0, IN_WIDTH = 0, 1024, 1280, 1536, 2560, 3584, 4608

LANES = 128
VMEM_LIMIT_BYTES = 56 * 1024 * 1024

TQ = 256
HALO_ROWS = 16
SB = 16
TB = 128
ROW_TILE = 8
CH = 256
CH_TILES = CH // ROW_TILE
CH_SHIFT = CH_TILES.bit_length() - 1
ZROWS = 256
TAIL_ROWS = -(-(N_EXPERTS * (ROW_TILE - 1) + CH) // ZROWS) * ZROWS
NEG_INF = float("-inf")


def _cparams(n_axes=1):
    return pltpu.CompilerParams(dimension_semantics=("arbitrary",) * n_axes,
                                vmem_limit_bytes=VMEM_LIMIT_BYTES)


def _layer_norm(r, g, b):
    mu = jnp.mean(r, axis=-1, keepdims=True)
    xc = r - mu
    var = jnp.mean(xc * xc, axis=-1, keepdims=True)
    return xc * lax.rsqrt(var + LN_EPS) * g + b


def _silu(x):
    return x * jax.nn.sigmoid(x)


def _pack_bf16_pairs(x):
    c = x.shape[1] // 2
    bits = lax.bitcast_convert_type(x.astype(BF16).astype(F32), U32)
    return (bits[:, :c] >> 16) | (bits[:, c:] & jnp.uint32(0xFFFF0000))


def _unpack_bf16_pairs(w):
    lo = lax.bitcast_convert_type(w << 16, F32)
    hi = lax.bitcast_convert_type(w & jnp.uint32(0xFFFF0000), F32)
    return lo, hi


def _rope128(c, ra, rb, rc):
    return c * ra + pltpu.roll(c, LANES - 8, 1) * rb + pltpu.roll(c, 8, 1) * rc


def _head_variants(x2):
    lo = lax.broadcasted_iota(I32, (1, LANES), 1) < HEAD_DIM
    sw = pltpu.roll(x2, HEAD_DIM, 1)
    zero = jnp.zeros_like(x2)
    return (jnp.where(lo, x2, zero).astype(BF16), jnp.where(lo, zero, sw).astype(BF16),
            jnp.where(lo, sw, zero).astype(BF16), jnp.where(lo, zero, x2).astype(BF16))


def _softmax_sink(s, extra, sink):
    m = jnp.maximum(jnp.max(s, axis=1, keepdims=True), sink)
    if extra is not None:
        m = jnp.maximum(m, extra)
    p = jnp.exp(s - m)
    den = jnp.sum(p, axis=1, keepdims=True) + jnp.exp(sink - m)
    if extra is None:
        return p * (1.0 / den), None
    pe = jnp.exp(extra - m)
    r = 1.0 / (den + pe)
    return p * r, pe * r


def _ada_kernel(c_ref, w_ref, b_ref, o_ref):
    s = _silu(c_ref[...]).astype(BF16)
    o_ref[...] = jnp.dot(s, w_ref[...].astype(BF16), preferred_element_type=F32) + b_ref[...]


def _ada(c_all, w_ada, b_ada):
    rows, n = c_all.shape[0], w_ada.shape[1]
    tn = 1536
    return pl.pallas_call(
        _ada_kernel,
        out_shape=jax.ShapeDtypeStruct((rows, n), F32),
        grid=(n // tn,),
        in_specs=[pl.BlockSpec((rows, D_MODEL), lambda j: (0, 0)),
                  pl.BlockSpec((D_MODEL, tn), lambda j: (0, j)),
                  pl.BlockSpec((1, tn), lambda j: (0, j))],
        out_specs=pl.BlockSpec((rows, tn), lambda j: (0, j)),
        compiler_params=_cparams(),
        name="ada",
    )(c_all, w_ada, b_ada.reshape(1, n))


def _mixer_prompt_kernel(sink_ref, x_ref, mods_ref, ra_ref, rb_ref, rc_ref, win_ref, wpool_ref,
                         pscale_ref, wout_ref, g1_ref, b1_ref,
                         x1_ref, h2_ref, kwin_ref, vwin_ref, pst_ref,
                         q_s, kk_s, vv_s, u_s, attn_s, pool_s):
    i = pl.program_id(0)
    last = pl.num_programs(0) - 1

    @pl.when(i == 0)
    def _():
        kk_s[:, :, 0:WINDOW, :] = jnp.zeros((N_KV_HEADS, 2, WINDOW, LANES), BF16)
        vv_s[:, :, 0:WINDOW, :] = jnp.zeros((N_KV_HEADS, 2, WINDOW, LANES), BF16)
        u_s[0:HALO_ROWS, :] = jnp.zeros((HALO_ROWS, D_MODEL), F32)

    shift1 = mods_ref[:, 0:1024]
    scale1 = mods_ref[:, 1024:2048]
    gate1 = mods_ref[:, 2048:3072]
    shift2 = mods_ref[:, 3072:4096]
    scale2 = mods_ref[:, 4096:5120]

    x = x_ref[...]
    h = (x * (1.0 + scale1) + shift1).astype(BF16)
    ra, rb, rc = ra_ref[...], rb_ref[...], rc_ref[...]

    q = jnp.dot(h, win_ref[:, Q0:K0], preferred_element_type=F32)
    for c in range(D_MODEL // LANES):
        qc = _rope128(q[:, c * LANES:(c + 1) * LANES], ra, rb, rc)
        q_s[:, c * LANES:(c + 1) * LANES] = (qc * (HEAD_DIM ** -0.5)).astype(BF16)

    k = jnp.dot(h, win_ref[:, K0:V0], preferred_element_type=F32)
    v = jnp.dot(h, win_ref[:, V0:U0], preferred_element_type=F32)
    for c in range(KV_WIDTH // LANES):
        kr = _rope128(k[:, c * LANES:(c + 1) * LANES], ra, rb, rc)
        vr = v[:, c * LANES:(c + 1) * LANES]

        @pl.when(i == last)
        def _(kr=kr, vr=vr, c=c):
            kwin_ref[:, c * LANES:(c + 1) * LANES] = kr[TQ - WINDOW:, :]
            vwin_ref[:, c * LANES:(c + 1) * LANES] = vr[TQ - WINDOW:, :]

        ka, kb, kc_, kd = _head_variants(kr)
        va, vb, vc_, vd = _head_variants(vr)
        kk_s[2 * c, 0, WINDOW:, :] = ka
        kk_s[2 * c, 1, WINDOW:, :] = kb
        kk_s[2 * c + 1, 0, WINDOW:, :] = kc_
        kk_s[2 * c + 1, 1, WINDOW:, :] = kd
        vv_s[2 * c, 0, WINDOW:, :] = va
        vv_s[2 * c, 1, WINDOW:, :] = vb
        vv_s[2 * c + 1, 0, WINDOW:, :] = vc_
        vv_s[2 * c + 1, 1, WINDOW:, :] = vd

    u_s[HALO_ROWS:, :] = jnp.dot(h, win_ref[:, U0:GA0], preferred_element_type=F32)

    @pl.when(i == last)
    def _():
        pst_ref[...] = u_s[HALO_ROWS + TQ - POOL_HALO:, :]

    rowpos = i * TQ + lax.broadcasted_iota(I32, (TQ, 1), 0)
    for g, w in enumerate(POOL_WINDOWS):
        cs = slice(g * POOL_GROUP_DIM, (g + 1) * POOL_GROUP_DIM)
        acc = u_s[HALO_ROWS:, cs]
        for m in range(1, w):
            acc = acc + u_s[HALO_ROWS - m:HALO_ROWS - m + TQ, cs]
        cnt = jnp.minimum(w, rowpos + 1).astype(F32)
        d = acc / cnt - u_s[HALO_ROWS:, cs]
        y = jnp.dot(d.astype(BF16), wpool_ref[g], preferred_element_type=F32)
        pool_s[:, cs] = y * pscale_ref[:, cs]

    a_idx = lax.broadcasted_iota(I32, (WINDOW, 4 * WINDOW), 0)
    j_idx = lax.broadcasted_iota(I32, (WINDOW, 4 * WINDOW), 1) & (2 * WINDOW - 1)
    no_prev = jnp.where(i > 0, 0, 4 * WINDOW)
    for qb in range(TQ // WINDOW):
        rows = slice(qb * WINDOW, (qb + 1) * WINDOW)
        band = slice(qb * WINDOW, (qb + 2) * WINDOW)
        a_prev = a_idx + no_prev if qb == 0 else a_idx
        mask = jnp.logical_or(jnp.logical_and(j_idx < WINDOW, j_idx >= a_prev),
                              jnp.logical_and(j_idx >= WINDOW, (j_idx - WINDOW) <= a_idx))
        for g in range(N_KV_HEADS):
            k2 = jnp.concatenate([kk_s[g, 0, band, :], kk_s[g, 1, band, :]], axis=0)
            v2 = jnp.concatenate([vv_s[g, 0, band, :], vv_s[g, 1, band, :]], axis=0)
            for p in range(2):
                c = 2 * g + p
                q2 = q_s[rows, c * LANES:(c + 1) * LANES]
                s2 = lax.dot_general(q2, k2, (((1,), (1,)), ((), ())), preferred_element_type=F32)
                s2 = jnp.where(mask, s2, NEG_INF)
                pa, _ = _softmax_sink(s2[:, :2 * WINDOW], None, sink_ref[2 * c])
                pb, _ = _softmax_sink(s2[:, 2 * WINDOW:], None, sink_ref[2 * c + 1])
                p2 = jnp.concatenate([pa, pb], axis=1).astype(BF16)
                attn_s[rows, c * LANES:(c + 1) * LANES] = jnp.dot(p2, v2, preferred_element_type=F32)

    kk_s[:, :, 0:WINDOW, :] = kk_s[:, :, TQ:, :]
    vv_s[:, :, 0:WINDOW, :] = vv_s[:, :, TQ:, :]
    u_s[0:HALO_ROWS, :] = u_s[TQ:, :]

    ga = jnp.dot(h, win_ref[:, GA0:GB0], preferred_element_type=F32)
    gb = jnp.dot(h, win_ref[:, GB0:IN_WIDTH], preferred_element_type=F32)
    merged = jax.nn.sigmoid(ga) * attn_s[...] + jax.nn.sigmoid(gb) * pool_s[...]
    y = jnp.dot(merged.astype(BF16), wout_ref[...], preferred_element_type=F32)
    x1 = _layer_norm(ALPHA * x + gate1 * y, g1_ref[...], b1_ref[...])
    x1_ref[...] = x1
    h2_ref[...] = _pack_bf16_pairs(x1 * (1.0 + scale2) + shift2)


def _mixer_prompt(x, mods_p, tabs, w_in_b, sinks, w_pool_b, pool_scale, w_out_b, ln1_g, ln1_b):
    t = x.shape[0]
    full = lambda shape: pl.BlockSpec(shape, lambda i: (0,) * len(shape))
    rowblk = lambda width: pl.BlockSpec((TQ, width), lambda i: (i, 0))
    return pl.pallas_call(
        _mixer_prompt_kernel,
        out_shape=(jax.ShapeDtypeStruct((t, D_MODEL), F32),
                   jax.ShapeDtypeStruct((t, D_MODEL // 2), U32),
                   jax.ShapeDtypeStruct((WINDOW, KV_WIDTH), F32),
                   jax.ShapeDtypeStruct((WINDOW, KV_WIDTH), F32),
                   jax.ShapeDtypeStruct((POOL_HALO, D_MODEL), F32)),
        grid=(t // TQ,),
        in_specs=[pl.BlockSpec(memory_space=pltpu.SMEM),
                  rowblk(D_MODEL), full((1, 6 * D_MODEL)),
                  rowblk(LANES), rowblk(LANES), rowblk(LANES),
                  full((D_MODEL, IN_WIDTH)), full((4, POOL_GROUP_DIM, POOL_GROUP_DIM)),
                  full((1, D_MODEL)), full((D_MODEL, D_MODEL)), full((1, D_MODEL)), full((1, D_MODEL))],
        out_specs=(rowblk(D_MODEL), rowblk(D_MODEL // 2),
                   full((WINDOW, KV_WIDTH)), full((WINDOW, KV_WIDTH)), full((POOL_HALO, D_MODEL))),
        scratch_shapes=[pltpu.VMEM((TQ, D_MODEL), BF16),
                        pltpu.VMEM((N_KV_HEADS, 2, WINDOW + TQ, LANES), BF16),
                        pltpu.VMEM((N_KV_HEADS, 2, WINDOW + TQ, LANES), BF16),
                        pltpu.VMEM((HALO_ROWS + TQ, D_MODEL), F32),
                        pltpu.VMEM((TQ, D_MODEL), F32),
                        pltpu.VMEM((TQ, D_MODEL), F32)],
        compiler_params=_cparams(),
        name="mixer_prompt",
    )(sinks, x, mods_p, *tabs, w_in_b, w_pool_b, pool_scale, w_out_b, ln1_g, ln1_b)


def _sample_proj_kernel(x_ref, mods_ref, ra_ref, rb_ref, rc_ref, win_ref, spt_ref, wpool_ref, pscale_ref,
                        q_ref, k_ref, v_ref, u_ref, pool_ref, ga_ref, gb_ref):
    x = x_ref[...]
    h = (x * (1.0 + mods_ref[:, 1024:2048]) + mods_ref[:, 0:1024]).astype(BF16)
    ra, rb, rc = ra_ref[...], rb_ref[...], rc_ref[...]
    q = jnp.dot(h, win_ref[:, Q0:K0], preferred_element_type=F32)
    for c in range(D_MODEL // LANES):
        qc = _rope128(q[:, c * LANES:(c + 1) * LANES], ra, rb, rc)
        q_ref[:, c * LANES:(c + 1) * LANES] = (qc * (HEAD_DIM ** -0.5)).astype(BF16)
    k = jnp.dot(h, win_ref[:, K0:V0], preferred_element_type=F32)
    for c in range(KV_WIDTH // LANES):
        k_ref[:, c * LANES:(c + 1) * LANES] = _rope128(k[:, c * LANES:(c + 1) * LANES], ra, rb, rc)
    v_ref[...] = jnp.dot(h, win_ref[:, V0:U0], preferred_element_type=F32)
    u = jnp.dot(h, win_ref[:, U0:GA0], preferred_element_type=F32)
    u_ref[...] = u
    for g, w in enumerate(POOL_WINDOWS):
        cs = slice(g * POOL_GROUP_DIM, (g + 1) * POOL_GROUP_DIM)
        acc = u[:, cs]
        for m in range(1, w):
            acc = acc + spt_ref[POOL_HALO - m, :, cs]
        d = acc / float(min(w, PAST_LEN + 1)) - u[:, cs]
        y = jnp.dot(d.astype(BF16), wpool_ref[g], preferred_element_type=F32)
        pool_ref[:, cs] = y * pscale_ref[:, cs]
    ga_ref[...] = jnp.dot(h, win_ref[:, GA0:GB0], preferred_element_type=F32)
    gb_ref[...] = jnp.dot(h, win_ref[:, GB0:IN_WIDTH], preferred_element_type=F32)


def _sample_proj(x_s, mods_s, tabs, w_in_b, sp_t, w_pool_b, pool_scale):
    b = x_s.shape[0]
    sd = lambda w, dt=F32: jax.ShapeDtypeStruct((b, w), dt)
    return pl.pallas_call(
        _sample_proj_kernel,
        out_shape=(sd(D_MODEL, BF16), sd(KV_WIDTH), sd(KV_WIDTH), sd(D_MODEL), sd(D_MODEL),
                   sd(D_MODEL), sd(D_MODEL)),
        compiler_params=pltpu.CompilerParams(vmem_limit_bytes=VMEM_LIMIT_BYTES),
        name="sample_proj",
    )(x_s, mods_s, *tabs, w_in_b, sp_t, w_pool_b, pool_scale)


def _sample_attn_kernel(sink_ref, q_ref, ck_ref, cv_ref, kn_ref, vn_ref, o_ref):
    nkeys = SB * WINDOW
    lo = lax.broadcasted_iota(I32, (1, LANES), 1) < HEAD_DIM
    row_b = lax.broadcasted_iota(I32, (2 * SB, 2 * nkeys), 0) & (SB - 1)
    col_b = (lax.broadcasted_iota(I32, (2 * SB, 2 * nkeys), 1) & (nkeys - 1)) >> 7
    own = row_b == col_b
    first_pair = lax.broadcasted_iota(I32, (2 * SB, 1), 0) < SB
    for c in range(KV_WIDTH // LANES):
        kvar = _head_variants(ck_ref[:, c * LANES:(c + 1) * LANES])
        vvar = _head_variants(cv_ref[:, c * LANES:(c + 1) * LANES])
        kn = kn_ref[:, c * LANES:(c + 1) * LANES].astype(BF16).astype(F32)
        vn = vn_ref[:, c * LANES:(c + 1) * LANES].astype(BF16).astype(F32)
        kn_sw, vn_sw = pltpu.roll(kn, HEAD_DIM, 1), pltpu.roll(vn, HEAD_DIM, 1)
        for e in range(2):
            g = 2 * c + e
            k2 = jnp.concatenate([kvar[2 * e], kvar[2 * e + 1]], axis=0)
            v2 = jnp.concatenate([vvar[2 * e], vvar[2 * e + 1]], axis=0)
            kdup = jnp.where(lo, kn, kn_sw) if e == 0 else jnp.where(lo, kn_sw, kn)
            vdup = jnp.where(lo, vn, vn_sw) if e == 0 else jnp.where(lo, vn_sw, vn)
            qa = q_ref[:, (2 * g) * LANES:(2 * g + 1) * LANES]
            qb = q_ref[:, (2 * g + 1) * LANES:(2 * g + 2) * LANES]
            q4 = jnp.concatenate([qa, qb], axis=0)
            s = lax.dot_general(q4, k2, (((1,), (1,)), ((), ())), preferred_element_type=F32)
            s = jnp.where(own, s, NEG_INF)
            prod = q4.astype(F32) * jnp.concatenate([kdup, kdup], axis=0)
            s_new = (jnp.sum(jnp.where(lo, prod, 0.0), axis=1, keepdims=True),
                     jnp.sum(jnp.where(lo, 0.0, prod), axis=1, keepdims=True))
            ps, pn = [], []
            for hh in range(2):
                sink = jnp.where(first_pair, sink_ref[4 * g + hh], sink_ref[4 * g + 2 + hh])
                p_c, p_n = _softmax_sink(s[:, hh * nkeys:(hh + 1) * nkeys], s_new[hh], sink)
                ps.append(p_c)
                pn.append(p_n)
            p2 = jnp.concatenate(ps, axis=1).astype(BF16)
            o = jnp.dot(p2, v2, preferred_element_type=F32)
            o = o + jnp.where(lo, pn[0], pn[1]) * jnp.concatenate([vdup, vdup], axis=0)
            o_ref[:, (2 * g) * LANES:(2 * g + 1) * LANES] = o[:SB]
            o_ref[:, (2 * g + 1) * LANES:(2 * g + 2) * LANES] = o[SB:]


def _sample_attn(sinks, q, ck2d, cv2d, k_new, v_new):
    b = q.shape[0]
    return pl.pallas_call(
        _sample_attn_kernel,
        out_shape=jax.ShapeDtypeStruct((b, D_MODEL), F32),
        grid=(b // SB,),
        in_specs=[pl.BlockSpec(memory_space=pltpu.SMEM),
                  pl.BlockSpec((SB, D_MODEL), lambda i: (i, 0)),
                  pl.BlockSpec((SB * WINDOW, KV_WIDTH), lambda i: (i, 0)),
                  pl.BlockSpec((SB * WINDOW, KV_WIDTH), lambda i: (i, 0)),
                  pl.BlockSpec((SB, KV_WIDTH), lambda i: (i, 0)),
                  pl.BlockSpec((SB, KV_WIDTH), lambda i: (i, 0))],
        out_specs=pl.BlockSpec((SB, D_MODEL), lambda i: (i, 0)),
        compiler_params=_cparams(),
        name="sample_attn",
    )(sinks, q, ck2d, cv2d, k_new, v_new)


def _sample_merge_kernel(x_ref, mods_ref, attn_ref, pool_ref, ga_ref, gb_ref, wout_ref, g1_ref, b1_ref,
                         x1_ref, h2_ref):
    merged = jax.nn.sigmoid(ga_ref[...]) * attn_ref[...] + jax.nn.sigmoid(gb_ref[...]) * pool_ref[...]
    y = jnp.dot(merged.astype(BF16), wout_ref[...], preferred_element_type=F32)
    x1 = _layer_norm(ALPHA * x_ref[...] + mods_ref[:, 2048:3072] * y, g1_ref[...], b1_ref[...])
    x1_ref[...] = x1
    h2_ref[...] = _pack_bf16_pairs(x1 * (1.0 + mods_ref[:, 4096:5120]) + mods_ref[:, 3072:4096])


def _sample_merge(x_s, mods_s, attn, pool, ga, gb, w_out_b, ln1_g, ln1_b):
    b = x_s.shape[0]
    return pl.pallas_call(
        _sample_merge_kernel,
        out_shape=(jax.ShapeDtypeStruct((b, D_MODEL), F32), jax.ShapeDtypeStruct((b, D_MODEL // 2), U32)),
        compiler_params=pltpu.CompilerParams(vmem_limit_bytes=VMEM_LIMIT_BYTES),
        name="sample_merge",
    )(x_s, mods_s, attn, pool, ga, gb, w_out_b, ln1_g, ln1_b)


def _router_kernel(h2_ref, wrt_ref, bias_ref, idx_ref, wts_ref, rank_ref, cnt_ref, carry_s):
    i = pl.program_id(0)

    @pl.when(i == 0)
    def _():
        carry_s[...] = jnp.zeros_like(carry_s)

    lo, hi = _unpack_bf16_pairs(h2_ref[...])
    nt = (((1,), (1,)), ((), ()))
    half = D_MODEL // 2
    logits = (lax.dot_general(wrt_ref[:, :half], lo.astype(BF16), nt, preferred_element_type=F32)
              + lax.dot_general(wrt_ref[:, half:], hi.astype(BF16), nt, preferred_element_type=F32))
    scores = jax.nn.sigmoid(logits)
    sel = scores + bias_ref[...]

    gi = lax.broadcasted_iota(I32, (EXPERTS_PER_GROUP, TB), 0).astype(F32)
    gscore = []
    for g in range(N_EXPERT_GROUPS):
        blk = sel[g * EXPERTS_PER_GROUP:(g + 1) * EXPERTS_PER_GROUP, :]
        m1 = jnp.max(blk, axis=0, keepdims=True)
        f1 = jnp.min(jnp.where(blk == m1, gi, float(EXPERTS_PER_GROUP)), axis=0, keepdims=True)
        m2 = jnp.max(jnp.where(gi == f1, NEG_INF, blk), axis=0, keepdims=True)
        gscore.append(m1 + m2)

    gsel = [jnp.zeros((1, TB), jnp.bool_) for _ in range(N_EXPERT_GROUPS)]
    for _ in range(TOPK_GROUPS):
        m = gscore[0]
        for g in range(1, N_EXPERT_GROUPS):
            m = jnp.maximum(m, gscore[g])
        found = jnp.zeros((1, TB), jnp.bool_)
        for g in range(N_EXPERT_GROUPS):
            pick = jnp.logical_and(gscore[g] == m, jnp.logical_not(found))
            found = jnp.logical_or(found, pick)
            gsel[g] = jnp.logical_or(gsel[g], pick)
            gscore[g] = jnp.where(pick, NEG_INF, gscore[g])
    cur = jnp.concatenate(
        [jnp.where(gsel[g], sel[g * EXPERTS_PER_GROUP:(g + 1) * EXPERTS_PER_GROUP, :], NEG_INF)
         for g in range(N_EXPERT_GROUPS)], axis=0)

    ei = lax.broadcasted_iota(I32, (N_EXPERTS, TB), 0).astype(F32)
    onehot = jnp.zeros((N_EXPERTS, TB), F32)
    picks, raw_w = [], []
    for _ in range(TOP_K):
        m = jnp.max(cur, axis=0, keepdims=True)
        f = jnp.min(jnp.where(cur == m, ei, float(N_EXPERTS)), axis=0, keepdims=True)
        pick = ei == f
        picks.append(f)
        raw_w.append(jnp.sum(jnp.where(pick, scores, 0.0), axis=0, keepdims=True))
        cur = jnp.where(pick, NEG_INF, cur)
        onehot = onehot + pick.astype(F32)
    wsum = raw_w[0]
    for k in range(1, TOP_K):
        wsum = wsum + raw_w[k]

    tri = (lax.broadcasted_iota(I32, (TB, TB), 0) <= lax.broadcasted_iota(I32, (TB, TB), 1)).astype(BF16)
    incl = jnp.dot(onehot.astype(BF16), tri, preferred_element_type=F32)
    pos = carry_s[...] + incl - onehot
    for k in range(TOP_K):
        idx_ref[k:k + 1, :] = picks[k].astype(I32)
        wts_ref[k:k + 1, :] = raw_w[k] / wsum * ROUTED_SCALE
        rank_ref[k:k + 1, :] = jnp.sum(jnp.where(ei == picks[k], pos, 0.0), axis=0, keepdims=True).astype(I32)
    carry = carry_s[...] + jnp.sum(onehot, axis=1, keepdims=True)
    carry_s[...] = carry
    cnt_ref[...] = carry


def _router(h2_all, w_router_t_b, router_bias):
    t_all = h2_all.shape[0]
    tok = lambda dt: jax.ShapeDtypeStruct((TOP_K, t_all), dt)
    return pl.pallas_call(
        _router_kernel,
        out_shape=(tok(I32), tok(F32), tok(I32), jax.ShapeDtypeStruct((N_EXPERTS, LANES), F32)),
        grid=(t_all // TB,),
        in_specs=[pl.BlockSpec((TB, D_MODEL // 2), lambda i: (i, 0)),
                  pl.BlockSpec((N_EXPERTS, D_MODEL), lambda i: (0, 0)),
                  pl.BlockSpec((N_EXPERTS, 1), lambda i: (0, 0))],
        out_specs=(pl.BlockSpec((TOP_K, TB), lambda i: (0, i)),
                   pl.BlockSpec((TOP_K, TB), lambda i: (0, i)),
                   pl.BlockSpec((TOP_K, TB), lambda i: (0, i)),
                   pl.BlockSpec((N_EXPERTS, LANES), lambda i: (0, 0))),
        scratch_shapes=[pltpu.VMEM((N_EXPERTS, LANES), F32)],
        compiler_params=_cparams(),
        name="router",
    )(h2_all, w_router_t_b, router_bias.reshape(N_EXPERTS, 1))


def _moe_meta_kernel(idx_ref, rank_ref, cnt_ref, dest_ref, emeta_ref, tstart_s):
    i = pl.program_id(0)

    @pl.when(i == 0)
    def _():
        cnt_col = cnt_ref[:, 0:1]
        tiles_col = jnp.floor((cnt_col + float(ROW_TILE - 1)) * (1.0 / ROW_TILE))
        e_r = lax.broadcasted_iota(I32, (N_EXPERTS, N_EXPERTS), 0)
        e_c = lax.broadcasted_iota(I32, (N_EXPERTS, N_EXPERTS), 1)
        diag = e_r == e_c
        tiles_row = jnp.sum(jnp.where(diag, tiles_col, 0.0), axis=0, keepdims=True)
        cnt_row = jnp.sum(jnp.where(diag, cnt_col, 0.0), axis=0, keepdims=True)
        tstart_row = jnp.sum(jnp.where(e_r < e_c, tiles_col, 0.0), axis=0, keepdims=True)
        tstart_col = jnp.sum(jnp.where(e_c < e_r, tiles_row, 0.0), axis=1, keepdims=True)
        tstart_s[...] = jnp.broadcast_to(tstart_col, (N_EXPERTS, LANES))
        emeta_ref[0:1, :] = tstart_row.astype(I32)
        emeta_ref[1:2, :] = tiles_row.astype(I32)
        emeta_ref[2:3, :] = cnt_row.astype(I32)
        later_used = jnp.logical_and(e_r > e_c, tiles_col > 0.0)
        next_used = jnp.min(jnp.where(later_used, e_r.astype(F32), float(N_EXPERTS)), axis=0, keepdims=True)
        emeta_ref[3:4, :] = next_used.astype(I32)
        emeta_ref[4:8, :] = jnp.zeros((4, N_EXPERTS), I32)

    ei = lax.broadcasted_iota(I32, (N_EXPERTS, TB), 0)
    tstart = tstart_s[...]
    for k in range(TOP_K):
        st = jnp.sum(jnp.where(ei == idx_ref[k:k + 1, :], tstart, 0.0), axis=0, keepdims=True)
        dest_ref[k:k + 1, :] = st.astype(I32) * ROW_TILE + rank_ref[k:k + 1, :]


def _moe_meta(idx, rank, counts):
    t_all = idx.shape[1]
    return pl.pallas_call(
        _moe_meta_kernel,
        out_shape=(jax.ShapeDtypeStruct((TOP_K, t_all), I32), jax.ShapeDtypeStruct((8, N_EXPERTS), I32)),
        grid=(t_all // TB,),
        in_specs=[pl.BlockSpec((TOP_K, TB), lambda i: (0, i)),
                  pl.BlockSpec((TOP_K, TB), lambda i: (0, i)),
                  pl.BlockSpec((N_EXPERTS, LANES), lambda i: (0, 0))],
        out_specs=(pl.BlockSpec((TOP_K, TB), lambda i: (0, i)),
                   pl.BlockSpec((8, N_EXPERTS), lambda i: (0, 0))),
        scratch_shapes=[pltpu.VMEM((N_EXPERTS, LANES), F32)],
        compiler_params=_cparams(),
        name="moe_meta",
    )(idx, rank, counts)


def _zero_tail(zbuf, out_ref, zsem, base):
    zbuf[...] = jnp.zeros_like(zbuf)
    copies = [pltpu.make_async_copy(zbuf, out_ref.at[pl.ds(base + j * ZROWS, ZROWS), :], zsem)
              for j in range(TAIL_ROWS // ZROWS)]
    for c in copies:
        c.start()
    for c in copies:
        c.wait()


def _dispatch_kernel(emeta_ref, dest_ref, h2_ref, xs_ref, zbuf, sem, zsem, *, base):
    i = pl.program_id(0)

    @pl.when(i == 0)
    def _():
        _zero_tail(zbuf, xs_ref, zsem, base)

        def gap_rows(e, fn):
            cnt = emeta_ref[2 * N_EXPERTS + e]
            first = emeta_ref[e] * ROW_TILE + cnt
            n_gap = emeta_ref[N_EXPERTS + e] * ROW_TILE - cnt

            def body(r, carry):
                fn(pltpu.make_async_copy(zbuf.at[pl.ds(0, 1), :], xs_ref.at[pl.ds(first + r, 1), :], zsem))
                return carry

            lax.fori_loop(0, n_gap, body, 0)

        def start_gaps(e, carry):
            gap_rows(e, lambda c: c.start())
            return carry

        def wait_gaps(e, carry):
            gap_rows(e, lambda c: c.wait())
            return carry

        lax.fori_loop(0, N_EXPERTS, start_gaps, 0)
        lax.fori_loop(0, N_EXPERTS, wait_gaps, 0)

    def row_copy(t, d):
        return pltpu.make_async_copy(h2_ref.at[pl.ds(t, 1), :], xs_ref.at[pl.ds(d, 1), :], sem)

    def start(t, carry):
        for k in range(TOP_K):
            row_copy(t, dest_ref[0, 0, k * TB + t]).start(priority=k % 2)
        return carry

    lax.fori_loop(0, TB, start, 0)

    def wait(t, carry):
        for k in range(TOP_K):
            row_copy(t, dest_ref[0, 0, k * TB + t]).wait()
        return carry

    lax.fori_loop(0, TB, wait, 0)


def _dispatch(emeta_flat, dest_blk, h2_all):
    t_all = h2_all.shape[0]
    base = t_all * TOP_K
    grid_spec = pltpu.PrefetchScalarGridSpec(
        num_scalar_prefetch=1,
        grid=(t_all // TB,),
        in_specs=[pl.BlockSpec((1, 1, TOP_K * TB), lambda i, m: (i, 0, 0), memory_space=pltpu.SMEM),
                  pl.BlockSpec((TB, D_MODEL // 2), lambda i, m: (i, 0))],
        out_specs=pl.BlockSpec(memory_space=pl.ANY),
        scratch_shapes=[pltpu.VMEM((ZROWS, D_MODEL // 2), U32),
                        pltpu.SemaphoreType.DMA(()), pltpu.SemaphoreType.DMA(())],
    )
    return pl.pallas_call(
        functools.partial(_dispatch_kernel, base=base),
        out_shape=jax.ShapeDtypeStruct((base + TAIL_ROWS, D_MODEL // 2), U32),
        grid_spec=grid_spec,
        compiler_params=_cparams(),
        name="dispatch",
    )(emeta_flat, dest_blk, h2_all)


def _experts_kernel(emeta_ref, xs_ref, wg_ref, wu_ref, wd_ref, ys_ref,
                    xbuf, ybuf, zbuf, wgu_s, wd_s, gcnt, isem, osem, zsem, *, base):
    e = pl.program_id(0)
    row0 = emeta_ref[e] * ROW_TILE
    n_tiles = emeta_ref[N_EXPERTS + e]
    n_chunks = (n_tiles + (CH_TILES - 1)) >> CH_SHIFT
    next_used = emeta_ref[3 * N_EXPERTS + e]

    def load(row, slot):
        return pltpu.make_async_copy(xs_ref.at[pl.ds(pl.multiple_of(row, ROW_TILE), CH), :],
                                     xbuf.at[slot], isem.at[slot])

    def stores(j, fn):
        valid = jnp.logical_and(j >= 0, j < n_chunks)
        full = jnp.logical_and(valid, (j + 1) * CH_TILES <= n_tiles)
        row = row0 + j * CH

        @pl.when(full)
        def _():
            fn(pltpu.make_async_copy(ybuf, ys_ref.at[pl.ds(pl.multiple_of(row, ROW_TILE), CH), :], osem))

        n_part = jnp.where(jnp.logical_and(valid, jnp.logical_not(full)), n_tiles - j * CH_TILES, 0)

        def body(t, carry):
            fn(pltpu.make_async_copy(
                ybuf.at[pl.ds(pl.multiple_of(t * ROW_TILE, ROW_TILE), ROW_TILE), :],
                ys_ref.at[pl.ds(pl.multiple_of(row + t * ROW_TILE, ROW_TILE), ROW_TILE), :], osem))
            return carry

        lax.fori_loop(0, n_part, body, 0)

    @pl.when(e == 0)
    def _():
        _zero_tail(zbuf, ys_ref, zsem, base)
        gcnt[0] = 0
        first_used = jnp.where(n_tiles > 0, 0, next_used)

        @pl.when(first_used < N_EXPERTS)
        def _():
            load(emeta_ref[first_used] * ROW_TILE, 0).start()

    @pl.when(n_tiles > 0)
    def _():
        wgu_s[:, :EXPERT_DIM] = wg_ref[...].astype(BF16)
        wgu_s[:, EXPERT_DIM:] = wu_ref[...].astype(BF16)
        wd_s[...] = wd_ref[...].astype(BF16)
        g0 = gcnt[0]

        def chunk(j, carry):
            slot = (g0 + j) & 1
            load(row0 + j * CH, slot).wait()

            @pl.when(j + 1 < n_chunks)
            def _():
                load(row0 + (j + 1) * CH, 1 - slot).start()

            @pl.when(jnp.logical_and(j + 1 == n_chunks, next_used < N_EXPERTS))
            def _():
                load(emeta_ref[next_used] * ROW_TILE, 1 - slot).start()

            lo, hi = _unpack_bf16_pairs(xbuf[slot])
            half = D_MODEL // 2
            gu = (jnp.dot(lo.astype(BF16), wgu_s[:half, :], preferred_element_type=F32)
                  + jnp.dot(hi.astype(BF16), wgu_s[half:, :], preferred_element_type=F32))
            act = _silu(gu[:, :EXPERT_DIM]) * gu[:, EXPERT_DIM:]
            y = _pack_bf16_pairs(jnp.dot(act.astype(BF16), wd_s[...], preferred_element_type=F32))
            stores(j - 1, lambda c: c.wait())
            ybuf[...] = y
            stores(j, lambda c: c.start())
            return carry

        lax.fori_loop(0, n_chunks, chunk, 0)
        stores(n_chunks - 1, lambda c: c.wait())
        gcnt[0] = g0 + n_chunks


def _experts(emeta_flat, xs, w_gate, w_up, w_down):
    base = xs.shape[0] - TAIL_ROWS
    w_map = lambda e, m: (e, 0, 0)
    grid_spec = pltpu.PrefetchScalarGridSpec(
        num_scalar_prefetch=1,
        grid=(N_EXPERTS,),
        in_specs=[pl.BlockSpec(memory_space=pl.ANY),
                  pl.BlockSpec((None, D_MODEL, EXPERT_DIM), w_map),
                  pl.BlockSpec((None, D_MODEL, EXPERT_DIM), w_map),
                  pl.BlockSpec((None, EXPERT_DIM, D_MODEL), w_map)],
        out_specs=pl.BlockSpec(memory_space=pl.ANY),
        scratch_shapes=[pltpu.VMEM((2, CH, D_MODEL // 2), U32),
                        pltpu.VMEM((CH, D_MODEL // 2), U32),
                        pltpu.VMEM((ZROWS, D_MODEL // 2), U32),
                        pltpu.VMEM((D_MODEL, 2 * EXPERT_DIM), BF16),
                        pltpu.VMEM((EXPERT_DIM, D_MODEL), BF16),
                        pltpu.SMEM((1,), I32),
                        pltpu.SemaphoreType.DMA((2,)), pltpu.SemaphoreType.DMA(()),
                        pltpu.SemaphoreType.DMA(())],
    )
    return pl.pallas_call(
        functools.partial(_experts_kernel, base=base),
        out_shape=jax.ShapeDtypeStruct(xs.shape, U32),
        grid_spec=grid_spec,
        compiler_params=_cparams(),
        name="experts",
    )(emeta_flat, xs, w_gate, w_up, w_down)


def _combine_kernel(dest_ref, dnext_ref, x1_ref, mods_ref, wts_ref, ys_ref, wsgu_ref, wsd_ref, g2_ref, b2_ref,
                    out_ref, buf, sem):
    i = pl.program_id(0)
    slot = i & 1

    def row_copy(dref, s, t, k):
        return pltpu.make_async_copy(ys_ref.at[pl.ds(dref[0, 0, k * TB + t], 1), :],
                                     buf.at[s, k, pl.ds(t, 1), :], sem.at[s])

    def gather(dref, s):
        def start(t, carry):
            for k in range(TOP_K):
                row_copy(dref, s, t, k).start(priority=k % 2)
            return carry

        lax.fori_loop(0, TB, start, 0)

    @pl.when(i == 0)
    def _():
        gather(dest_ref, 0)

    @pl.when(i + 1 < pl.num_programs(0))
    def _():
        gather(dnext_ref, 1 - slot)

    x1 = x1_ref[...]
    h2 = (x1 * (1.0 + mods_ref[:, 4096:5120]) + mods_ref[:, 3072:4096]).astype(BF16)
    gu = jnp.dot(h2, wsgu_ref[...], preferred_element_type=F32)
    act = _silu(gu[:, :SHARED_DIM]) * gu[:, SHARED_DIM:]
    shared = jnp.dot(act.astype(BF16), wsd_ref[...], preferred_element_type=F32)

    def wait(t, carry):
        for k in range(TOP_K):
            row_copy(dest_ref, slot, t, k).wait()
        return carry

    lax.fori_loop(0, TB, wait, 0)

    acc_lo = jnp.zeros((TB, D_MODEL // 2), F32)
    acc_hi = jnp.zeros((TB, D_MODEL // 2), F32)
    for k in range(TOP_K):
        lo, hi = _unpack_bf16_pairs(buf[slot, k])
        wk = wts_ref[:, k:k + 1]
        acc_lo = acc_lo + wk * lo
        acc_hi = acc_hi + wk * hi
    m = shared + jnp.concatenate([acc_lo, acc_hi], axis=1)
    out_ref[...] = _layer_norm(ALPHA * x1 + mods_ref[:, 5120:6144] * m, g2_ref[...], b2_ref[...])


def _combine(dest_blk, x1, mods, wts_t, ys, wsgu_b, wsd_b, ln2_g, ln2_b, blk_off):
    t = x1.shape[0]
    per_row = mods.shape[0] != 1
    mods_spec = (pl.BlockSpec((TB, 6 * D_MODEL), lambda i: (i, 0)) if per_row
                 else pl.BlockSpec((1, 6 * D_MODEL), lambda i: (0, 0)))
    full = lambda shape: pl.BlockSpec(shape, lambda i: (0,) * len(shape))
    n_steps = t // TB
    return pl.pallas_call(
        _combine_kernel,
        out_shape=jax.ShapeDtypeStruct((t, D_MODEL), F32),
        grid=(n_steps,),
        in_specs=[pl.BlockSpec((1, 1, TOP_K * TB), lambda i: (i + blk_off, 0, 0), memory_space=pltpu.SMEM),
                  pl.BlockSpec((1, 1, TOP_K * TB), lambda i: (jnp.minimum(i + 1, n_steps - 1) + blk_off, 0, 0),
                               memory_space=pltpu.SMEM),
                  pl.BlockSpec((TB, D_MODEL), lambda i: (i, 0)),
                  mods_spec,
                  pl.BlockSpec((TB, TOP_K), lambda i: (i + blk_off, 0)),
                  pl.BlockSpec(memory_space=pl.ANY),
                  full((D_MODEL, 2 * SHARED_DIM)), full((SHARED_DIM, D_MODEL)),
                  full((1, D_MODEL)), full((1, D_MODEL))],
        out_specs=pl.BlockSpec((TB, D_MODEL), lambda i: (i, 0)),
        scratch_shapes=[pltpu.VMEM((2, TOP_K, TB, D_MODEL // 2), U32), pltpu.SemaphoreType.DMA((2,))],
        compiler_params=_cparams(),
        name="combine",
    )(dest_blk, dest_blk, x1, mods, wts_t, ys, wsgu_b, wsd_b, ln2_g, ln2_b)


def _rope_tables(pos):
    half = ROT_DIM // 2
    inv = ROPE_THETA ** (-jnp.arange(0, ROT_DIM, 2, dtype=F32) / ROT_DIM)
    ang = pos.astype(F32)[:, None] * inv[None, :]
    cos, sin = jnp.cos(ang), jnp.sin(ang)
    n = pos.shape[0]
    ones = jnp.ones((n, HEAD_DIM - ROT_DIM), F32)
    z = lambda w: jnp.zeros((n, w), F32)
    ra = jnp.concatenate([cos, cos, ones], axis=1)
    rb = jnp.concatenate([-sin, z(HEAD_DIM - half)], axis=1)
    rc = jnp.concatenate([z(half), sin, z(HEAD_DIM - ROT_DIM)], axis=1)
    return tuple(jnp.tile(t, (1, LANES // HEAD_DIM)) for t in (ra, rb, rc))


def kernel(x_prompt, x_sample, cache_k, cache_v, state_pool, c_prompt, c_sample, w_ada, b_ada, w_in, attn_sinks, w_pool, pool_scale, w_out, ln1_g, ln1_b, w_router, router_bias, w_exp_gate, w_exp_up, w_exp_down, w_sh_gate, w_sh_up, w_sh_down, ln2_g, ln2_b):
    assert w_ada.shape[0] == DEPTH == 1 and x_prompt.shape[0] == 1 and x_sample.shape[1] == 1
    t_p, n_s = x_prompt.shape[1], x_sample.shape[0]
    assert t_p % TQ == 0 and n_s % SB == 0 and n_s == TB and cache_k.shape[2] == WINDOW
    t_all = t_p + n_s

    row = lambda a: a[0].reshape(1, -1)
    w_in_b, w_out_b, w_pool_b = w_in[0].astype(BF16), w_out[0].astype(BF16), w_pool[0].astype(BF16)
    wsgu_b = jnp.concatenate([w_sh_gate[0], w_sh_up[0]], axis=1).astype(BF16)
    wsd_b = w_sh_down[0].astype(BF16)
    w_router_t_b = w_router[0].T.astype(BF16)
    sinks = attn_sinks[0]
    pscale, g1, b1, g2, b2 = row(pool_scale), row(ln1_g), row(ln1_b), row(ln2_g), row(ln2_b)

    x_s = x_sample.reshape(n_s, D_MODEL)
    c_all = jnp.concatenate([c_sample, c_prompt, jnp.zeros((7, D_MODEL), F32)], axis=0)
    mods = _ada(c_all, w_ada[0], b_ada[0])
    mods_s, mods_p = mods[:n_s], mods[n_s:n_s + 1]

    x_p = x_prompt.reshape(t_p, D_MODEL)
    x1_p, h2_p, k_win, v_win, pool_st = _mixer_prompt(
        x_p, mods_p, _rope_tables(jnp.arange(t_p)), w_in_b, sinks, w_pool_b, pscale, w_out_b, g1, b1)

    sp_t = jnp.transpose(state_pool[0], (1, 0, 2))
    q_s, k_new, v_new, u_new, pool_s, ga_s, gb_s = _sample_proj(
        x_s, mods_s, _rope_tables(jnp.full((1,), PAST_LEN)), w_in_b, sp_t, w_pool_b, pscale)
    attn_s = _sample_attn(sinks, q_s, cache_k[0].reshape(n_s * WINDOW, KV_WIDTH),
                          cache_v[0].reshape(n_s * WINDOW, KV_WIDTH), k_new, v_new)
    x1_s, h2_s = _sample_merge(x_s, mods_s, attn_s, pool_s, ga_s, gb_s, w_out_b, g1, b1)

    h2_all = jnp.concatenate([h2_p, h2_s], axis=0)
    idx, wts, rank, counts = _router(h2_all, w_router_t_b, router_bias[0])
    dest, emeta = _moe_meta(idx, rank, counts)
    emeta_flat = emeta[:4].reshape(-1)
    n_tb = t_all // TB
    dest_blk = dest.reshape(TOP_K, n_tb, TB).transpose(1, 0, 2).reshape(n_tb, 1, TOP_K * TB)
    xs = _dispatch(emeta_flat, dest_blk, h2_all)
    ys = _experts(emeta_flat, xs, w_exp_gate[0], w_exp_up[0], w_exp_down[0])
    wts_t = wts.T
    y_p = _combine(dest_blk, x1_p, mods_p, wts_t, ys, wsgu_b, wsd_b, g2, b2, 0)
    y_s = _combine(dest_blk, x1_s, mods_s, wts_t, ys, wsgu_b, wsd_b, g2, b2, t_p // TB)

    k_win_s = jnp.concatenate([cache_k[0][:, 1:], k_new.reshape(n_s, 1, N_KV_HEADS, HEAD_DIM)], axis=1)
    v_win_s = jnp.concatenate([cache_v[0][:, 1:], v_new.reshape(n_s, 1, N_KV_HEADS, HEAD_DIM)], axis=1)
    pool_s_out = jnp.concatenate([state_pool[0][:, 1:], u_new[:, None, :]], axis=1)
    return (y_p.reshape(1, t_p, D_MODEL), y_s.reshape(n_s, 1, D_MODEL),
            k_win.reshape(1, 1, WINDOW, N_KV_HEADS, HEAD_DIM), v_win.reshape(1, 1, WINDOW, N_KV_HEADS, HEAD_DIM),
            pool_st.reshape(1, 1, POOL_HALO, D_MODEL),
            k_win_s[None], v_win_s[None], pool_s_out[None])
```

```python
import functools

import jax
import jax.numpy as jnp
from jax import lax
from jax.experimental import pallas as pl
from jax.experimental.pallas import tpu as pltpu

F32, BF16, I32, U32 = jnp.float32, jnp.bfloat16, jnp.int32, jnp.uint32

D_MODEL = 1024
N_HEADS, HEAD_DIM, N_KV_HEADS = 16, 64, 4
KV_WIDTH = N_KV_HEADS * HEAD_DIM
WINDOW = 128
ROPE_THETA, ROT_DIM = 500000.0, 16
PAST_LEN = 16384
POOL_WINDOWS = (2, 4, 8, 16)
POOL_GROUP_DIM = 256
POOL_HALO = 15
N_EXPERTS, TOP_K, N_EXPERT_GROUPS, TOPK_GROUPS = 256, 8, 8, 4
EXPERTS_PER_GROUP = N_EXPERTS // N_EXPERT_GROUPS
EXPERT_DIM, SHARED_DIM = 256, 256
ROUTED_SCALE = 2.5
LN_EPS = 1e-5
DEPTH = 1
ALPHA = (2.0 * DEPTH) ** 0.25
Q0, K0, V0, U0, GA0, GB0, IN_WIDTH = 0, 1024, 1280, 1536, 2560, 3584, 4608

LANES = 128
VMEM_LIMIT_BYTES = 56 * 1024 * 1024

TQ = 256
HALO_ROWS = 16
SB = 16
TB = 128
ROW_TILE = 8
CH = 256
CH_TILES = CH // ROW_TILE
CH_SHIFT = CH_TILES.bit_length() - 1
N_XBUF = 4
ZROWS = 256
TAIL_ROWS = -(-(N_EXPERTS * (ROW_TILE - 1) + CH) // ZROWS) * ZROWS
NEG_INF = float("-inf")


def _cparams(n_axes=1):
    return pltpu.CompilerParams(dimension_semantics=("arbitrary",) * n_axes,
                                vmem_limit_bytes=VMEM_LIMIT_BYTES)


def _layer_norm(r, g, b):
    mu = jnp.mean(r, axis=-1, keepdims=True)
    xc = r - mu
    var = jnp.mean(xc * xc, axis=-1, keepdims=True)
    return xc * lax.rsqrt(var + LN_EPS) * g + b


def _silu(x):
    return x * jax.nn.sigmoid(x)


def _pack_bf16_pairs(x):
    c = x.shape[1] // 2
    bits = lax.bitcast_convert_type(x.astype(BF16).astype(F32), U32)
    return (bits[:, :c] >> 16) | (bits[:, c:] & jnp.uint32(0xFFFF0000))


def _unpack_bf16_pairs(w):
    lo = lax.bitcast_convert_type(w << 16, F32)
    hi = lax.bitcast_convert_type(w & jnp.uint32(0xFFFF0000), F32)
    return lo, hi


def _rope128(c, ra, rb, rc):
    return c * ra + pltpu.roll(c, LANES - 8, 1) * rb + pltpu.roll(c, 8, 1) * rc


def _head_variants(x2):
    lo = lax.broadcasted_iota(I32, (1, LANES), 1) < HEAD_DIM
    sw = pltpu.roll(x2, HEAD_DIM, 1)
    zero = jnp.zeros_like(x2)
    return (jnp.where(lo, x2, zero).astype(BF16), jnp.where(lo, zero, sw).astype(BF16),
            jnp.where(lo, sw, zero).astype(BF16), jnp.where(lo, zero, x2).astype(BF16))


def _softmax_sink(s, extra, sink):
    m = jnp.maximum(jnp.max(s, axis=1, keepdims=True), sink)
    if extra is not None:
        m = jnp.maximum(m, extra)
    p = jnp.exp(s - m)
    den = jnp.sum(p, axis=1, keepdims=True) + jnp.exp(sink - m)
    if extra is None:
        return p * (1.0 / den), None
    pe = jnp.exp(extra - m)
    r = 1.0 / (den + pe)
    return p * r, pe * r


def _ada_kernel(c_ref, w_ref, b_ref, o_ref):
    s = _silu(c_ref[...]).astype(BF16)
    o_ref[...] = jnp.dot(s, w_ref[...].astype(BF16), preferred_element_type=F32) + b_ref[...]


def _ada(c_all, w_ada, b_ada):
    rows, n = c_all.shape[0], w_ada.shape[1]
    tn = 1536
    return pl.pallas_call(
        _ada_kernel,
        out_shape=jax.ShapeDtypeStruct((rows, n), F32),
        grid=(n // tn,),
        in_specs=[pl.BlockSpec((rows, D_MODEL), lambda j: (0, 0)),
                  pl.BlockSpec((D_MODEL, tn), lambda j: (0, j)),
                  pl.BlockSpec((1, tn), lambda j: (0, j))],
        out_specs=pl.BlockSpec((rows, tn), lambda j: (0, j)),
        compiler_params=_cparams(),
        name="ada",
    )(c_all, w_ada, b_ada.reshape(1, n))


def _mixer_prompt_kernel(sink_ref, x_ref, mods_ref, ra_ref, rb_ref, rc_ref, win_ref, wpool_ref,
                         pscale_ref, wout_ref, g1_ref, b1_ref,
                         x1_ref, h2_ref, kwin_ref, vwin_ref, pst_ref,
                         q_s, kk_s, vv_s, u_s, attn_s, pool_s):
    i = pl.program_id(0)
    last = pl.num_programs(0) - 1

    @pl.when(i == 0)
    def _():
        kk_s[:, :, 0:WINDOW, :] = jnp.zeros((N_KV_HEADS, 2, WINDOW, LANES), BF16)
        vv_s[:, :, 0:WINDOW, :] = jnp.zeros((N_KV_HEADS, 2, WINDOW, LANES), BF16)
        u_s[0:HALO_ROWS, :] = jnp.zeros((HALO_ROWS, D_MODEL), F32)

    shift1 = mods_ref[:, 0:1024]
    scale1 = mods_ref[:, 1024:2048]
    gate1 = mods_ref[:, 2048:3072]
    shift2 = mods_ref[:, 3072:4096]
    scale2 = mods_ref[:, 4096:5120]

    x = x_ref[...]
    h = (x * (1.0 + scale1) + shift1).astype(BF16)
    ra, rb, rc = ra_ref[...], rb_ref[...], rc_ref[...]

    q = jnp.dot(h, win_ref[:, Q0:K0], preferred_element_type=F32)
    for c in range(D_MODEL // LANES):
        qc = _rope128(q[:, c * LANES:(c + 1) * LANES], ra, rb, rc)
        q_s[:, c * LANES:(c + 1) * LANES] = (qc * (HEAD_DIM ** -0.5)).astype(BF16)

    k = jnp.dot(h, win_ref[:, K0:V0], preferred_element_type=F32)
    v = jnp.dot(h, win_ref[:, V0:U0], preferred_element_type=F32)
    for c in range(KV_WIDTH // LANES):
        kr = _rope128(k[:, c * LANES:(c + 1) * LANES], ra, rb, rc)
        vr = v[:, c * LANES:(c + 1) * LANES]

        @pl.when(i == last)
        def _(kr=kr, vr=vr, c=c):
            kwin_ref[:, c * LANES:(c + 1) * LANES] = kr[TQ - WINDOW:, :]
            vwin_ref[:, c * LANES:(c + 1) * LANES] = vr[TQ - WINDOW:, :]

        ka, kb, kc_, kd = _head_variants(kr)
        va, vb, vc_, vd = _head_variants(vr)
        kk_s[2 * c, 0, WINDOW:, :] = ka
        kk_s[2 * c, 1, WINDOW:, :] = kb
        kk_s[2 * c + 1, 0, WINDOW:, :] = kc_
        kk_s[2 * c + 1, 1, WINDOW:, :] = kd
        vv_s[2 * c, 0, WINDOW:, :] = va
        vv_s[2 * c, 1, WINDOW:, :] = vb
        vv_s[2 * c + 1, 0, WINDOW:, :] = vc_
        vv_s[2 * c + 1, 1, WINDOW:, :] = vd

    u_s[HALO_ROWS:, :] = jnp.dot(h, win_ref[:, U0:GA0], preferred_element_type=F32)

    @pl.when(i == last)
    def _():
        pst_ref[...] = u_s[HALO_ROWS + TQ - POOL_HALO:, :]

    rowpos = i * TQ + lax.broadcasted_iota(I32, (TQ, 1), 0)
    for g, w in enumerate(POOL_WINDOWS):
        cs = slice(g * POOL_GROUP_DIM, (g + 1) * POOL_GROUP_DIM)
        acc = u_s[HALO_ROWS:, cs]
        for m in range(1, w):
            acc = acc + u_s[HALO_ROWS - m:HALO_ROWS - m + TQ, cs]
        cnt = jnp.minimum(w, rowpos + 1).astype(F32)
        d = acc / cnt - u_s[HALO_ROWS:, cs]
        y = jnp.dot(d.astype(BF16), wpool_ref[g], preferred_element_type=F32)
        pool_s[:, cs] = y * pscale_ref[:, cs]

    a_idx = lax.broadcasted_iota(I32, (WINDOW, 4 * WINDOW), 0)
    j_idx = lax.broadcasted_iota(I32, (WINDOW, 4 * WINDOW), 1) & (2 * WINDOW - 1)
    no_prev = jnp.where(i > 0, 0, 4 * WINDOW)
    lo_lanes = lax.broadcasted_iota(I32, (1, LANES), 1) < HEAD_DIM
    key_half = lax.broadcasted_iota(I32, (4 * WINDOW, LANES), 0) >> 8
    lane_half = lax.broadcasted_iota(I32, (4 * WINDOW, LANES), 1) >> 6
    head_ones = (key_half == lane_half).astype(BF16)
    for qb in range(TQ // WINDOW):
        rows = slice(qb * WINDOW, (qb + 1) * WINDOW)
        band = slice(qb * WINDOW, (qb + 2) * WINDOW)
        a_prev = a_idx + no_prev if qb == 0 else a_idx
        mask = jnp.logical_or(jnp.logical_and(j_idx < WINDOW, j_idx >= a_prev),
                              jnp.logical_and(j_idx >= WINDOW, (j_idx - WINDOW) <= a_idx))
        for g in range(N_KV_HEADS):
            k2 = jnp.concatenate([kk_s[g, 0, band, :], kk_s[g, 1, band, :]], axis=0)
            v2 = jnp.concatenate([vv_s[g, 0, band, :], vv_s[g, 1, band, :]], axis=0)
            for p in range(2):
                c = 2 * g + p
                q2 = q_s[rows, c * LANES:(c + 1) * LANES]
                s2 = lax.dot_general(q2, k2, (((1,), (1,)), ((), ())), preferred_element_type=F32)
                s2 = jnp.where(mask, s2, NEG_INF)
                sink_a, sink_b = sink_ref[2 * c], sink_ref[2 * c + 1]
                ma = jnp.maximum(jnp.max(s2[:, :2 * WINDOW], axis=1, keepdims=True), sink_a)
                mb = jnp.maximum(jnp.max(s2[:, 2 * WINDOW:], axis=1, keepdims=True), sink_b)
                p2 = jnp.concatenate([jnp.exp(s2[:, :2 * WINDOW] - ma), jnp.exp(s2[:, 2 * WINDOW:] - mb)],
                                     axis=1).astype(BF16)
                o2 = jnp.dot(p2, v2, preferred_element_type=F32)
                den = jnp.dot(p2, head_ones, preferred_element_type=F32)
                den = den + jnp.where(lo_lanes, jnp.exp(sink_a - ma), jnp.exp(sink_b - mb))
                attn_s[rows, c * LANES:(c + 1) * LANES] = o2 * (1.0 / den)

    kk_s[:, :, 0:WINDOW, :] = kk_s[:, :, TQ:, :]
    vv_s[:, :, 0:WINDOW, :] = vv_s[:, :, TQ:, :]
    u_s[0:HALO_ROWS, :] = u_s[TQ:, :]

    ga = jnp.dot(h, win_ref[:, GA0:GB0], preferred_element_type=F32)
    gb = jnp.dot(h, win_ref[:, GB0:IN_WIDTH], preferred_element_type=F32)
    merged = jax.nn.sigmoid(ga) * attn_s[...] + jax.nn.sigmoid(gb) * pool_s[...]
    y = jnp.dot(merged.astype(BF16), wout_ref[...], preferred_element_type=F32)
    x1 = _layer_norm(ALPHA * x + gate1 * y, g1_ref[...], b1_ref[...])
    x1_ref[...] = x1
    h2_ref[...] = _pack_bf16_pairs(x1 * (1.0 + scale2) + shift2)


def _mixer_prompt(x, mods_p, tabs, w_in_b, sinks, w_pool_b, pool_scale, w_out_b, ln1_g, ln1_b):
    t = x.shape[0]
    full = lambda shape: pl.BlockSpec(shape, lambda i: (0,) * len(shape))
    rowblk = lambda width: pl.BlockSpec((TQ, width), lambda i: (i, 0))
    return pl.pallas_call(
        _mixer_prompt_kernel,
        out_shape=(jax.ShapeDtypeStruct((t, D_MODEL), F32),
                   jax.ShapeDtypeStruct((t, D_MODEL // 2), U32),
                   jax.ShapeDtypeStruct((WINDOW, KV_WIDTH), F32),
                   jax.ShapeDtypeStruct((WINDOW, KV_WIDTH), F32),
                   jax.ShapeDtypeStruct((POOL_HALO, D_MODEL), F32)),
        grid=(t // TQ,),
        in_specs=[pl.BlockSpec(memory_space=pltpu.SMEM),
                  rowblk(D_MODEL), full((1, 6 * D_MODEL)),
                  rowblk(LANES), rowblk(LANES), rowblk(LANES),
                  full((D_MODEL, IN_WIDTH)), full((4, POOL_GROUP_DIM, POOL_GROUP_DIM)),
                  full((1, D_MODEL)), full((D_MODEL, D_MODEL)), full((1, D_MODEL)), full((1, D_MODEL))],
        out_specs=(rowblk(D_MODEL), rowblk(D_MODEL // 2),
                   full((WINDOW, KV_WIDTH)), full((WINDOW, KV_WIDTH)), full((POOL_HALO, D_MODEL))),
        scratch_shapes=[pltpu.VMEM((TQ, D_MODEL), BF16),
                        pltpu.VMEM((N_KV_HEADS, 2, WINDOW + TQ, LANES), BF16),
                        pltpu.VMEM((N_KV_HEADS, 2, WINDOW + TQ, LANES), BF16),
                        pltpu.VMEM((HALO_ROWS + TQ, D_MODEL), F32),
                        pltpu.VMEM((TQ, D_MODEL), F32),
                        pltpu.VMEM((TQ, D_MODEL), F32)],
        compiler_params=_cparams(),
        name="mixer_prompt",
    )(sinks, x, mods_p, *tabs, w_in_b, w_pool_b, pool_scale, w_out_b, ln1_g, ln1_b)


def _sample_proj_kernel(x_ref, mods_ref, ra_ref, rb_ref, rc_ref, win_ref, spt_ref, wpool_ref, pscale_ref,
                        q_ref, k_ref, v_ref, u_ref, pool_ref, ga_ref, gb_ref):
    x = x_ref[...]
    h = (x * (1.0 + mods_ref[:, 1024:2048]) + mods_ref[:, 0:1024]).astype(BF16)
    ra, rb, rc = ra_ref[...], rb_ref[...], rc_ref[...]
    q = jnp.dot(h, win_ref[:, Q0:K0], preferred_element_type=F32)
    for c in range(D_MODEL // LANES):
        qc = _rope128(q[:, c * LANES:(c + 1) * LANES], ra, rb, rc)
        q_ref[:, c * LANES:(c + 1) * LANES] = (qc * (HEAD_DIM ** -0.5)).astype(BF16)
    k = jnp.dot(h, win_ref[:, K0:V0], preferred_element_type=F32)
    for c in range(KV_WIDTH // LANES):
        k_ref[:, c * LANES:(c + 1) * LANES] = _rope128(k[:, c * LANES:(c + 1) * LANES], ra, rb, rc)
    v_ref[...] = jnp.dot(h, win_ref[:, V0:U0], preferred_element_type=F32)
    u = jnp.dot(h, win_ref[:, U0:GA0], preferred_element_type=F32)
    u_ref[...] = u
    for g, w in enumerate(POOL_WINDOWS):
        cs = slice(g * POOL_GROUP_DIM, (g + 1) * POOL_GROUP_DIM)
        acc = u[:, cs]
        for m in range(1, w):
            acc = acc + spt_ref[POOL_HALO - m, :, cs]
        d = acc / float(min(w, PAST_LEN + 1)) - u[:, cs]
        y = jnp.dot(d.astype(BF16), wpool_ref[g], preferred_element_type=F32)
        pool_ref[:, cs] = y * pscale_ref[:, cs]
    ga_ref[...] = jnp.dot(h, win_ref[:, GA0:GB0], preferred_element_type=F32)
    gb_ref[...] = jnp.dot(h, win_ref[:, GB0:IN_WIDTH], preferred_element_type=F32)


def _sample_proj(x_s, mods_s, tabs, w_in_b, sp_t, w_pool_b, pool_scale):
    b = x_s.shape[0]
    sd = lambda w, dt=F32: jax.ShapeDtypeStruct((b, w), dt)
    return pl.pallas_call(
        _sample_proj_kernel,
        out_shape=(sd(D_MODEL, BF16), sd(KV_WIDTH), sd(KV_WIDTH), sd(D_MODEL), sd(D_MODEL),
                   sd(D_MODEL), sd(D_MODEL)),
        compiler_params=pltpu.CompilerParams(vmem_limit_bytes=VMEM_LIMIT_BYTES),
        name="sample_proj",
    )(x_s, mods_s, *tabs, w_in_b, sp_t, w_pool_b, pool_scale)


def _sample_attn_kernel(sink_ref, q_ref, ck_ref, cv_ref, kn_ref, vn_ref, o_ref):
    nkeys = SB * WINDOW
    lo = lax.broadcasted_iota(I32, (1, LANES), 1) < HEAD_DIM
    row_b = lax.broadcasted_iota(I32, (2 * SB, 2 * nkeys), 0) & (SB - 1)
    col_b = (lax.broadcasted_iota(I32, (2 * SB, 2 * nkeys), 1) & (nkeys - 1)) >> 7
    own = row_b == col_b
    first_pair = lax.broadcasted_iota(I32, (2 * SB, 1), 0) < SB
    for c in range(KV_WIDTH // LANES):
        kvar = _head_variants(ck_ref[:, c * LANES:(c + 1) * LANES])
        vvar = _head_variants(cv_ref[:, c * LANES:(c + 1) * LANES])
        kn = kn_ref[:, c * LANES:(c + 1) * LANES].astype(BF16).astype(F32)
        vn = vn_ref[:, c * LANES:(c + 1) * LANES].astype(BF16).astype(F32)
        kn_sw, vn_sw = pltpu.roll(kn, HEAD_DIM, 1), pltpu.roll(vn, HEAD_DIM, 1)
        for e in range(2):
            g = 2 * c + e
            k2 = jnp.concatenate([kvar[2 * e], kvar[2 * e + 1]], axis=0)
            v2 = jnp.concatenate([vvar[2 * e], vvar[2 * e + 1]], axis=0)
            kdup = jnp.where(lo, kn, kn_sw) if e == 0 else jnp.where(lo, kn_sw, kn)
            vdup = jnp.where(lo, vn, vn_sw) if e == 0 else jnp.where(lo, vn_sw, vn)
            qa = q_ref[:, (2 * g) * LANES:(2 * g + 1) * LANES]
            qb = q_ref[:, (2 * g + 1) * LANES:(2 * g + 2) * LANES]
            q4 = jnp.concatenate([qa, qb], axis=0)
            s = lax.dot_general(q4, k2, (((1,), (1,)), ((), ())), preferred_element_type=F32)
            s = jnp.where(own, s, NEG_INF)
            prod = q4.astype(F32) * jnp.concatenate([kdup, kdup], axis=0)
            s_new = (jnp.sum(jnp.where(lo, prod, 0.0), axis=1, keepdims=True),
                     jnp.sum(jnp.where(lo, 0.0, prod), axis=1, keepdims=True))
            ps, pn = [], []
            for hh in range(2):
                sink = jnp.where(first_pair, sink_ref[4 * g + hh], sink_ref[4 * g + 2 + hh])
                p_c, p_n = _softmax_sink(s[:, hh * nkeys:(hh + 1) * nkeys], s_new[hh], sink)
                ps.append(p_c)
                pn.append(p_n)
            p2 = jnp.concatenate(ps, axis=1).astype(BF16)
            o = jnp.dot(p2, v2, preferred_element_type=F32)
            o = o + jnp.where(lo, pn[0], pn[1]) * jnp.concatenate([vdup, vdup], axis=0)
            o_ref[:, (2 * g) * LANES:(2 * g + 1) * LANES] = o[:SB]
            o_ref[:, (2 * g + 1) * LANES:(2 * g + 2) * LANES] = o[SB:]


def _sample_attn(sinks, q, ck2d, cv2d, k_new, v_new):
    b = q.shape[0]
    return pl.pallas_call(
        _sample_attn_kernel,
        out_shape=jax.ShapeDtypeStruct((b, D_MODEL), F32),
        grid=(b // SB,),
        in_specs=[pl.BlockSpec(memory_space=pltpu.SMEM),
                  pl.BlockSpec((SB, D_MODEL), lambda i: (i, 0)),
                  pl.BlockSpec((SB * WINDOW, KV_WIDTH), lambda i: (i, 0)),
                  pl.BlockSpec((SB * WINDOW, KV_WIDTH), lambda i: (i, 0)),
                  pl.BlockSpec((SB, KV_WIDTH), lambda i: (i, 0)),
                  pl.BlockSpec((SB, KV_WIDTH), lambda i: (i, 0))],
        out_specs=pl.BlockSpec((SB, D_MODEL), lambda i: (i, 0)),
        compiler_params=_cparams(),
        name="sample_attn",
    )(sinks, q, ck2d, cv2d, k_new, v_new)


def _sample_merge_kernel(x_ref, mods_ref, attn_ref, pool_ref, ga_ref, gb_ref, wout_ref, g1_ref, b1_ref,
                         x1_ref, h2_ref):
    merged = jax.nn.sigmoid(ga_ref[...]) * attn_ref[...] + jax.nn.sigmoid(gb_ref[...]) * pool_ref[...]
    y = jnp.dot(merged.astype(BF16), wout_ref[...], preferred_element_type=F32)
    x1 = _layer_norm(ALPHA * x_ref[...] + mods_ref[:, 2048:3072] * y, g1_ref[...], b1_ref[...])
    x1_ref[...] = x1
    h2_ref[...] = _pack_bf16_pairs(x1 * (1.0 + mods_ref[:, 4096:5120]) + mods_ref[:, 3072:4096])


def _sample_merge(x_s, mods_s, attn, pool, ga, gb, w_out_b, ln1_g, ln1_b):
    b = x_s.shape[0]
    return pl.pallas_call(
        _sample_merge_kernel,
        out_shape=(jax.ShapeDtypeStruct((b, D_MODEL), F32), jax.ShapeDtypeStruct((b, D_MODEL // 2), U32)),
        compiler_params=pltpu.CompilerParams(vmem_limit_bytes=VMEM_LIMIT_BYTES),
        name="sample_merge",
    )(x_s, mods_s, attn, pool, ga, gb, w_out_b, ln1_g, ln1_b)


def _router_kernel(h2_ref, wrt_ref, bias_ref, idx_ref, wts_ref, rank_ref, cnt_ref, carry_s):
    i = pl.program_id(0)

    @pl.when(i == 0)
    def _():
        carry_s[...] = jnp.zeros_like(carry_s)

    lo, hi = _unpack_bf16_pairs(h2_ref[...])
    nt = (((1,), (1,)), ((), ()))
    half = D_MODEL // 2
    logits = (lax.dot_general(wrt_ref[:, :half], lo.astype(BF16), nt, preferred_element_type=F32)
              + lax.dot_general(wrt_ref[:, half:], hi.astype(BF16), nt, preferred_element_type=F32))
    scores = jax.nn.sigmoid(logits)
    sel = scores + bias_ref[...]

    gi = lax.broadcasted_iota(I32, (EXPERTS_PER_GROUP, TB), 0).astype(F32)
    gscore = []
    for g in range(N_EXPERT_GROUPS):
        blk = sel[g * EXPERTS_PER_GROUP:(g + 1) * EXPERTS_PER_GROUP, :]
        m1 = jnp.max(blk, axis=0, keepdims=True)
        f1 = jnp.min(jnp.where(blk == m1, gi, float(EXPERTS_PER_GROUP)), axis=0, keepdims=True)
        m2 = jnp.max(jnp.where(gi == f1, NEG_INF, blk), axis=0, keepdims=True)
        gscore.append(m1 + m2)

    gsel = [jnp.zeros((1, TB), jnp.bool_) for _ in range(N_EXPERT_GROUPS)]
    for _ in range(TOPK_GROUPS):
        m = gscore[0]
        for g in range(1, N_EXPERT_GROUPS):
            m = jnp.maximum(m, gscore[g])
        found = jnp.zeros((1, TB), jnp.bool_)
        for g in range(N_EXPERT_GROUPS):
            pick = jnp.logical_and(gscore[g] == m, jnp.logical_not(found))
            found = jnp.logical_or(found, pick)
            gsel[g] = jnp.logical_or(gsel[g], pick)
            gscore[g] = jnp.where(pick, NEG_INF, gscore[g])
    cur = jnp.concatenate(
        [jnp.where(gsel[g], sel[g * EXPERTS_PER_GROUP:(g + 1) * EXPERTS_PER_GROUP, :], NEG_INF)
         for g in range(N_EXPERT_GROUPS)], axis=0)

    ei = lax.broadcasted_iota(I32, (N_EXPERTS, TB), 0).astype(F32)
    onehot = jnp.zeros((N_EXPERTS, TB), F32)
    picks, raw_w = [], []
    for _ in range(TOP_K):
        m = jnp.max(cur, axis=0, keepdims=True)
        f = jnp.min(jnp.where(cur == m, ei, float(N_EXPERTS)), axis=0, keepdims=True)
        pick = ei == f
        picks.append(f)
        raw_w.append(jnp.sum(jnp.where(pick, scores, 0.0), axis=0, keepdims=True))
        cur = jnp.where(pick, NEG_INF, cur)
        onehot = onehot + pick.astype(F32)
    wsum = raw_w[0]
    for k in range(1, TOP_K):
        wsum = wsum + raw_w[k]

    tri = (lax.broadcasted_iota(I32, (TB, TB), 0) <= lax.broadcasted_iota(I32, (TB, TB), 1)).astype(BF16)
    incl = jnp.dot(onehot.astype(BF16), tri, preferred_element_type=F32)
    pos = carry_s[...] + incl - onehot
    for k in range(TOP_K):
        idx_ref[k:k + 1, :] = picks[k].astype(I32)
        wts_ref[k:k + 1, :] = raw_w[k] / wsum * ROUTED_SCALE
        rank_ref[k:k + 1, :] = jnp.sum(jnp.where(ei == picks[k], pos, 0.0), axis=0, keepdims=True).astype(I32)
    carry = carry_s[...] + jnp.sum(onehot, axis=1, keepdims=True)
    carry_s[...] = carry
    cnt_ref[...] = carry


def _router(h2_all, w_router_t_b, router_bias):
    t_all = h2_all.shape[0]
    tok = lambda dt: jax.ShapeDtypeStruct((TOP_K, t_all), dt)
    return pl.pallas_call(
        _router_kernel,
        out_shape=(tok(I32), tok(F32), tok(I32), jax.ShapeDtypeStruct((N_EXPERTS, LANES), F32)),
        grid=(t_all // TB,),
        in_specs=[pl.BlockSpec((TB, D_MODEL // 2), lambda i: (i, 0)),
                  pl.BlockSpec((N_EXPERTS, D_MODEL), lambda i: (0, 0)),
                  pl.BlockSpec((N_EXPERTS, 1), lambda i: (0, 0))],
        out_specs=(pl.BlockSpec((TOP_K, TB), lambda i: (0, i)),
                   pl.BlockSpec((TOP_K, TB), lambda i: (0, i)),
                   pl.BlockSpec((TOP_K, TB), lambda i: (0, i)),
                   pl.BlockSpec((N_EXPERTS, LANES), lambda i: (0, 0))),
        scratch_shapes=[pltpu.VMEM((N_EXPERTS, LANES), F32)],
        compiler_params=_cparams(),
        name="router",
    )(h2_all, w_router_t_b, router_bias.reshape(N_EXPERTS, 1))


def _moe_meta_kernel(idx_ref, rank_ref, cnt_ref, dest_ref, emeta_ref, tstart_s):
    i = pl.program_id(0)

    @pl.when(i == 0)
    def _():
        cnt_col = cnt_ref[:, 0:1]
        tiles_col = jnp.floor((cnt_col + float(ROW_TILE - 1)) * (1.0 / ROW_TILE))
        e_r = lax.broadcasted_iota(I32, (N_EXPERTS, N_EXPERTS), 0)
        e_c = lax.broadcasted_iota(I32, (N_EXPERTS, N_EXPERTS), 1)
        diag = e_r == e_c
        tiles_row = jnp.sum(jnp.where(diag, tiles_col, 0.0), axis=0, keepdims=True)
        cnt_row = jnp.sum(jnp.where(diag, cnt_col, 0.0), axis=0, keepdims=True)
        tstart_row = jnp.sum(jnp.where(e_r < e_c, tiles_col, 0.0), axis=0, keepdims=True)
        tstart_col = jnp.sum(jnp.where(e_c < e_r, tiles_row, 0.0), axis=1, keepdims=True)
        tstart_s[...] = jnp.broadcast_to(tstart_col, (N_EXPERTS, LANES))
        emeta_ref[0:1, :] = tstart_row.astype(I32)
        emeta_ref[1:2, :] = tiles_row.astype(I32)
        emeta_ref[2:3, :] = cnt_row.astype(I32)
        later_used = jnp.logical_and(e_r > e_c, tiles_col > 0.0)
        next_used = jnp.min(jnp.where(later_used, e_r.astype(F32), float(N_EXPERTS)), axis=0, keepdims=True)
        emeta_ref[3:4, :] = next_used.astype(I32)
        emeta_ref[4:8, :] = jnp.zeros((4, N_EXPERTS), I32)

    ei = lax.broadcasted_iota(I32, (N_EXPERTS, TB), 0)
    tstart = tstart_s[...]
    for k in range(TOP_K):
        st = jnp.sum(jnp.where(ei == idx_ref[k:k + 1, :], tstart, 0.0), axis=0, keepdims=True)
        dest_ref[k:k + 1, :] = st.astype(I32) * ROW_TILE + rank_ref[k:k + 1, :]


def _moe_meta(idx, rank, counts):
    t_all = idx.shape[1]
    return pl.pallas_call(
        _moe_meta_kernel,
        out_shape=(jax.ShapeDtypeStruct((TOP_K, t_all), I32), jax.ShapeDtypeStruct((8, N_EXPERTS), I32)),
        grid=(t_all // TB,),
        in_specs=[pl.BlockSpec((TOP_K, TB), lambda i: (0, i)),
                  pl.BlockSpec((TOP_K, TB), lambda i: (0, i)),
                  pl.BlockSpec((N_EXPERTS, LANES), lambda i: (0, 0))],
        out_specs=(pl.BlockSpec((TOP_K, TB), lambda i: (0, i)),
                   pl.BlockSpec((8, N_EXPERTS), lambda i: (0, 0))),
        scratch_shapes=[pltpu.VMEM((N_EXPERTS, LANES), F32)],
        compiler_params=_cparams(),
        name="moe_meta",
    )(idx, rank, counts)


def _zero_tail(zbuf, out_ref, zsem, base):
    zbuf[...] = jnp.zeros_like(zbuf)
    copies = [pltpu.make_async_copy(zbuf, out_ref.at[pl.ds(base + j * ZROWS, ZROWS), :], zsem)
              for j in range(TAIL_ROWS // ZROWS)]
    for c in copies:
        c.start()
    for c in copies:
        c.wait()


def _dispatch_kernel(emeta_ref, dest_ref, h2_ref, xs_ref, zbuf, sem, zsem, *, base):
    i = pl.program_id(0)

    @pl.when(i == 0)
    def _():
        _zero_tail(zbuf, xs_ref, zsem, base)

        def gap_rows(e, fn):
            cnt = emeta_ref[2 * N_EXPERTS + e]
            first = emeta_ref[e] * ROW_TILE + cnt
            n_gap = emeta_ref[N_EXPERTS + e] * ROW_TILE - cnt

            def body(r, carry):
                fn(pltpu.make_async_copy(zbuf.at[pl.ds(0, 1), :], xs_ref.at[pl.ds(first + r, 1), :], zsem))
                return carry

            lax.fori_loop(0, n_gap, body, 0)

        def start_gaps(e, carry):
            gap_rows(e, lambda c: c.start())
            return carry

        def wait_gaps(e, carry):
            gap_rows(e, lambda c: c.wait())
            return carry

        lax.fori_loop(0, N_EXPERTS, start_gaps, 0)
        lax.fori_loop(0, N_EXPERTS, wait_gaps, 0)

    def row_copy(a, b, k):
        d = dest_ref[0, 0, k * TB + a * ROW_TILE + b]
        return pltpu.make_async_copy(h2_ref.at[a, pl.ds(b, 1), :], xs_ref.at[pl.ds(d, 1), :], sem)

    def for_rows(fn):
        def body(a, carry):
            for b in range(ROW_TILE):
                for k in range(TOP_K):
                    fn(row_copy(a, b, k), k)
            return carry

        lax.fori_loop(0, TB // ROW_TILE, body, 0)

    for_rows(lambda c, k: c.start(priority=k % 2))
    for_rows(lambda c, k: c.wait())


def _dispatch(emeta_flat, dest_blk, h2_all):
    t_all = h2_all.shape[0]
    base = t_all * TOP_K
    grid_spec = pltpu.PrefetchScalarGridSpec(
        num_scalar_prefetch=1,
        grid=(t_all // TB,),
        in_specs=[pl.BlockSpec((1, 1, TOP_K * TB), lambda i, m: (i, 0, 0), memory_space=pltpu.SMEM),
                  pl.BlockSpec((TB // ROW_TILE, ROW_TILE, D_MODEL // 2), lambda i, m: (i, 0, 0))],
        out_specs=pl.BlockSpec(memory_space=pl.ANY),
        scratch_shapes=[pltpu.VMEM((ZROWS, D_MODEL // 2), U32),
                        pltpu.SemaphoreType.DMA(()), pltpu.SemaphoreType.DMA(())],
    )
    return pl.pallas_call(
        functools.partial(_dispatch_kernel, base=base),
        out_shape=jax.ShapeDtypeStruct((base + TAIL_ROWS, D_MODEL // 2), U32),
        grid_spec=grid_spec,
        compiler_params=_cparams(),
        name="dispatch",
    )(emeta_flat, dest_blk, h2_all.reshape(t_all // ROW_TILE, ROW_TILE, D_MODEL // 2))


def _experts_kernel(emeta_ref, xs_ref, wg_ref, wu_ref, wd_ref, ys_ref,
                    xbuf, ybuf, zbuf, wgu_s, wd_s, st, pend_row, pend_n, isem, osem, zsem, *, base):
    e = pl.program_id(0)
    row0 = emeta_ref[e] * ROW_TILE
    n_tiles = emeta_ref[N_EXPERTS + e]
    n_chunks = (n_tiles + (CH_TILES - 1)) >> CH_SHIFT

    def load(row, slot):
        return pltpu.make_async_copy(xs_ref.at[pl.ds(pl.multiple_of(row, ROW_TILE), CH), :],
                                     xbuf.at[slot], isem.at[slot])

    def request_next(slot):
        ce, cj = st[1], st[2]

        @pl.when(ce < N_EXPERTS)
        def _():
            load(emeta_ref[ce] * ROW_TILE + cj * CH, slot).start()
            c_chunks = (emeta_ref[N_EXPERTS + ce] + (CH_TILES - 1)) >> CH_SHIFT
            more = cj + 1 < c_chunks
            st[1] = jnp.where(more, ce, emeta_ref[3 * N_EXPERTS + ce])
            st[2] = jnp.where(more, cj + 1, 0)

    def out_copies(slot, fn):
        row, n = pend_row[slot], pend_n[slot]

        @pl.when(n == CH_TILES + 1)
        def _():
            fn(pltpu.make_async_copy(ybuf.at[slot], ys_ref.at[pl.ds(pl.multiple_of(row, ROW_TILE), CH), :],
                                     osem.at[slot]))

        def body(t, carry):
            fn(pltpu.make_async_copy(
                ybuf.at[slot, pl.ds(pl.multiple_of(t * ROW_TILE, ROW_TILE), ROW_TILE), :],
                ys_ref.at[pl.ds(pl.multiple_of(row + t * ROW_TILE, ROW_TILE), ROW_TILE), :], osem.at[slot]))
            return carry

        lax.fori_loop(0, jnp.where(n <= CH_TILES, n, 0), body, 0)

    @pl.when(e == 0)
    def _():
        _zero_tail(zbuf, ys_ref, zsem, base)
        st[0] = 0
        st[1] = jnp.where(n_tiles > 0, 0, emeta_ref[3 * N_EXPERTS])
        st[2] = 0
        pend_n[0] = 0
        pend_n[1] = 0
        for s in range(N_XBUF - 1):
            request_next(s)

    @pl.when(n_tiles > 0)
    def _():
        wgu_s[:, :EXPERT_DIM] = wg_ref[...].astype(BF16)
        wgu_s[:, EXPERT_DIM:] = wu_ref[...].astype(BF16)
        wd_s[...] = wd_ref[...].astype(BF16)
        g0 = st[0]

        def chunk(j, carry):
            g = g0 + j
            slot = g & (N_XBUF - 1)
            load(row0 + j * CH, slot).wait()
            request_next((g + N_XBUF - 1) & (N_XBUF - 1))

            lo, hi = _unpack_bf16_pairs(xbuf[slot])
            half = D_MODEL // 2
            gu = (jnp.dot(lo.astype(BF16), wgu_s[:half, :], preferred_element_type=F32)
                  + jnp.dot(hi.astype(BF16), wgu_s[half:, :], preferred_element_type=F32))
            act = _silu(gu[:, :EXPERT_DIM]) * gu[:, EXPERT_DIM:]
            y = _pack_bf16_pairs(jnp.dot(act.astype(BF16), wd_s[...], preferred_element_type=F32))

            oslot = g & 1
            out_copies(oslot, lambda c: c.wait())
            ybuf[oslot] = y
            pend_row[oslot] = row0 + j * CH
            pend_n[oslot] = jnp.where((j + 1) * CH_TILES <= n_tiles, CH_TILES + 1, n_tiles - j * CH_TILES)
            out_copies(oslot, lambda c: c.start())
            return carry

        lax.fori_loop(0, n_chunks, chunk, 0)
        st[0] = g0 + n_chunks

    @pl.when(e == pl.num_programs(0) - 1)
    def _():
        for s in range(2):
            out_copies(s, lambda c: c.wait())
            pend_n[s] = 0


def _experts(emeta_flat, xs, w_gate, w_up, w_down):
    base = xs.shape[0] - TAIL_ROWS
    w_map = lambda e, m: (e, 0, 0)
    grid_spec = pltpu.PrefetchScalarGridSpec(
        num_scalar_prefetch=1,
        grid=(N_EXPERTS,),
        in_specs=[pl.BlockSpec(memory_space=pl.ANY),
                  pl.BlockSpec((None, D_MODEL, EXPERT_DIM), w_map),
                  pl.BlockSpec((None, D_MODEL, EXPERT_DIM), w_map),
                  pl.BlockSpec((None, EXPERT_DIM, D_MODEL), w_map)],
        out_specs=pl.BlockSpec(memory_space=pl.ANY),
        scratch_shapes=[pltpu.VMEM((N_XBUF, CH, D_MODEL // 2), U32),
                        pltpu.VMEM((2, CH, D_MODEL // 2), U32),
                        pltpu.VMEM((ZROWS, D_MODEL // 2), U32),
                        pltpu.VMEM((D_MODEL, 2 * EXPERT_DIM), BF16),
                        pltpu.VMEM((EXPERT_DIM, D_MODEL), BF16),
                        pltpu.SMEM((3,), I32), pltpu.SMEM((2,), I32), pltpu.SMEM((2,), I32),
                        pltpu.SemaphoreType.DMA((N_XBUF,)), pltpu.SemaphoreType.DMA((2,)),
                        pltpu.SemaphoreType.DMA(())],
    )
    return pl.pallas_call(
        functools.partial(_experts_kernel, base=base),
        out_shape=jax.ShapeDtypeStruct(xs.shape, U32),
        grid_spec=grid_spec,
        compiler_params=_cparams(),
        name="experts",
    )(emeta_flat, xs, w_gate, w_up, w_down)


def _combine_kernel(dest_ref, dnext_ref, x1_ref, mods_ref, wts_ref, ys_ref, wsgu_ref, wsd_ref, g2_ref, b2_ref,
                    out_ref, buf, sem):
    i = pl.program_id(0)
    slot = i & 1

    def row_copy(dref, s, a, b, k):
        return pltpu.make_async_copy(ys_ref.at[pl.ds(dref[0, 0, k * TB + a * ROW_TILE + b], 1), :],
                                     buf.at[s, k, a, pl.ds(b, 1), :], sem.at[s])

    def for_rows(fn):
        def body(a, carry):
            for b in range(ROW_TILE):
                for k in range(TOP_K):
                    fn(a, b, k)
            return carry

        lax.fori_loop(0, TB // ROW_TILE, body, 0)

    def gather(dref, s):
        for_rows(lambda a, b, k: row_copy(dref, s, a, b, k).start(priority=k % 2))

    @pl.when(i == 0)
    def _():
        gather(dest_ref, 0)

    @pl.when(i + 1 < pl.num_programs(0))
    def _():
        gather(dnext_ref, 1 - slot)

    x1 = x1_ref[...]
    h2 = (x1 * (1.0 + mods_ref[:, 4096:5120]) + mods_ref[:, 3072:4096]).astype(BF16)
    gu = jnp.dot(h2, wsgu_ref[...], preferred_element_type=F32)
    act = _silu(gu[:, :SHARED_DIM]) * gu[:, SHARED_DIM:]
    shared = jnp.dot(act.astype(BF16), wsd_ref[...], preferred_element_type=F32)

    for_rows(lambda a, b, k: row_copy(dest_ref, slot, a, b, k).wait())

    acc_lo = jnp.zeros((TB, D_MODEL // 2), F32)
    acc_hi = jnp.zeros((TB, D_MODEL // 2), F32)
    for k in range(TOP_K):
        lo, hi = _unpack_bf16_pairs(buf[slot, k].reshape(TB, D_MODEL // 2))
        wk = wts_ref[:, k:k + 1]
        acc_lo = acc_lo + wk * lo
        acc_hi = acc_hi + wk * hi
    m = shared + jnp.concatenate([acc_lo, acc_hi], axis=1)
    out_ref[...] = _layer_norm(ALPHA * x1 + mods_ref[:, 5120:6144] * m, g2_ref[...], b2_ref[...])


def _combine(dest_blk, x1, mods, wts_t, ys, wsgu_b, wsd_b, ln2_g, ln2_b, blk_off):
    t = x1.shape[0]
    per_row = mods.shape[0] != 1
    mods_spec = (pl.BlockSpec((TB, 6 * D_MODEL), lambda i: (i, 0)) if per_row
                 else pl.BlockSpec((1, 6 * D_MODEL), lambda i: (0, 0)))
    full = lambda shape: pl.BlockSpec(shape, lambda i: (0,) * len(shape))
    n_steps = t // TB
    return pl.pallas_call(
        _combine_kernel,
        out_shape=jax.ShapeDtypeStruct((t, D_MODEL), F32),
        grid=(n_steps,),
        in_specs=[pl.BlockSpec((1, 1, TOP_K * TB), lambda i: (i + blk_off, 0, 0), memory_space=pltpu.SMEM),
                  pl.BlockSpec((1, 1, TOP_K * TB), lambda i: (jnp.minimum(i + 1, n_steps - 1) + blk_off, 0, 0),
                               memory_space=pltpu.SMEM),
                  pl.BlockSpec((TB, D_MODEL), lambda i: (i, 0)),
                  mods_spec,
                  pl.BlockSpec((TB, TOP_K), lambda i: (i + blk_off, 0)),
                  pl.BlockSpec(memory_space=pl.ANY),
                  full((D_MODEL, 2 * SHARED_DIM)), full((SHARED_DIM, D_MODEL)),
                  full((1, D_MODEL)), full((1, D_MODEL))],
        out_specs=pl.BlockSpec((TB, D_MODEL), lambda i: (i, 0)),
        scratch_shapes=[pltpu.VMEM((2, TOP_K, TB // ROW_TILE, ROW_TILE, D_MODEL // 2), U32),
                        pltpu.SemaphoreType.DMA((2,))],
        compiler_params=_cparams(),
        name="combine",
    )(dest_blk, dest_blk, x1, mods, wts_t, ys, wsgu_b, wsd_b, ln2_g, ln2_b)


def _rope_tables(pos):
    half = ROT_DIM // 2
    inv = ROPE_THETA ** (-jnp.arange(0, ROT_DIM, 2, dtype=F32) / ROT_DIM)
    ang = pos.astype(F32)[:, None] * inv[None, :]
    cos, sin = jnp.cos(ang), jnp.sin(ang)
    n = pos.shape[0]
    ones = jnp.ones((n, HEAD_DIM - ROT_DIM), F32)
    z = lambda w: jnp.zeros((n, w), F32)
    ra = jnp.concatenate([cos, cos, ones], axis=1)
    rb = jnp.concatenate([-sin, z(HEAD_DIM - half)], axis=1)
    rc = jnp.concatenate([z(half), sin, z(HEAD_DIM - ROT_DIM)], axis=1)
    return tuple(jnp.tile(t, (1, LANES // HEAD_DIM)) for t in (ra, rb, rc))


def kernel(x_prompt, x_sample, cache_k, cache_v, state_pool, c_prompt, c_sample, w_ada, b_ada, w_in, attn_sinks, w_pool, pool_scale, w_out, ln1_g, ln1_b, w_router, router_bias, w_exp_gate, w_exp_up, w_exp_down, w_sh_gate, w_sh_up, w_sh_down, ln2_g, ln2_b):
    assert w_ada.shape[0] == DEPTH == 1 and x_prompt.shape[0] == 1 and x_sample.shape[1] == 1
    t_p, n_s = x_prompt.shape[1], x_sample.shape[0]
    assert t_p % TQ == 0 and n_s % SB == 0 and n_s == TB and cache_k.shape[2] == WINDOW
    t_all = t_p + n_s

    row = lambda a: a[0].reshape(1, -1)
    w_in_b, w_out_b, w_pool_b = w_in[0].astype(BF16), w_out[0].astype(BF16), w_pool[0].astype(BF16)
    wsgu_b = jnp.concatenate([w_sh_gate[0], w_sh_up[0]], axis=1).astype(BF16)
    wsd_b = w_sh_down[0].astype(BF16)
    w_router_t_b = w_router[0].T.astype(BF16)
    sinks = attn_sinks[0]
    pscale, g1, b1, g2, b2 = row(pool_scale), row(ln1_g), row(ln1_b), row(ln2_g), row(ln2_b)

    x_s = x_sample.reshape(n_s, D_MODEL)
    c_all = jnp.concatenate([c_sample, c_prompt, jnp.zeros((7, D_MODEL), F32)], axis=0)
    mods = _ada(c_all, w_ada[0], b_ada[0])
    mods_s, mods_p = mods[:n_s], mods[n_s:n_s + 1]

    x_p = x_prompt.reshape(t_p, D_MODEL)
    x1_p, h2_p, k_win, v_win, pool_st = _mixer_prompt(
        x_p, mods_p, _rope_tables(jnp.arange(t_p)), w_in_b, sinks, w_pool_b, pscale, w_out_b, g1, b1)

    sp_t = jnp.transpose(state_pool[0], (1, 0, 2))
    q_s, k_new, v_new, u_new, pool_s, ga_s, gb_s = _sample_proj(
        x_s, mods_s, _rope_tables(jnp.full((1,), PAST_LEN)), w_in_b, sp_t, w_pool_b, pscale)
    attn_s = _sample_attn(sinks, q_s, cache_k[0].reshape(n_s * WINDOW, KV_WIDTH),
                          cache_v[0].reshape(n_s * WINDOW, KV_WIDTH), k_new, v_new)
    x1_s, h2_s = _sample_merge(x_s, mods_s, attn_s, pool_s, ga_s, gb_s, w_out_b, g1, b1)

    h2_all = jnp.concatenate([h2_p, h2_s], axis=0)
    idx, wts, rank, counts = _router(h2_all, w_router_t_b, router_bias[0])
    dest, emeta = _moe_meta(idx, rank, counts)
    emeta_flat = emeta[:4].reshape(-1)
    n_tb = t_all // TB
    dest_blk = dest.reshape(TOP_K, n_tb, TB).transpose(1, 0, 2).reshape(n_tb, 1, TOP_K * TB)
    xs = _dispatch(emeta_flat, dest_blk, h2_all)
    ys = _experts(emeta_flat, xs, w_exp_gate[0], w_exp_up[0], w_exp_down[0])
    wts_t = wts.T
    y_p = _combine(dest_blk, x1_p, mods_p, wts_t, ys, wsgu_b, wsd_b, g2, b2, 0)
    y_s = _combine(dest_blk, x1_s, mods_s, wts_t, ys, wsgu_b, wsd_b, g2, b2, t_p // TB)

    k_win_s = jnp.concatenate([cache_k[0][:, 1:], k_new.reshape(n_s, 1, N_KV_HEADS, HEAD_DIM)], axis=1)
    v_win_s = jnp.concatenate([cache_v[0][:, 1:], v_new.reshape(n_s, 1, N_KV_HEADS, HEAD_DIM)], axis=1)
    pool_s_out = jnp.concatenate([state_pool[0][:, 1:], u_new[:, None, :]], axis=1)
    return (y_p.reshape(1, t_p, D_MODEL), y_s.reshape(n_s, 1, D_MODEL),
            k_win.reshape(1, 1, WINDOW, N_KV_HEADS, HEAD_DIM), v_win.reshape(1, 1, WINDOW, N_KV_HEADS, HEAD_DIM),
            pool_st.reshape(1, 1, POOL_HALO, D_MODEL),
            k_win_s[None], v_win_s[None], pool_s_out[None])
```

```python
import functools

import jax
import jax.numpy as jnp
from jax import lax
from jax.experimental import pallas as pl
from jax.experimental.pallas import tpu as pltpu

F32, BF16, I32, U32 = jnp.float32, jnp.bfloat16, jnp.int32, jnp.uint32

D_MODEL = 1024
N_HEADS, HEAD_DIM, N_KV_HEADS = 16, 64, 4
KV_WIDTH = N_KV_HEADS * HEAD_DIM
WINDOW = 128
ROPE_THETA, ROT_DIM = 500000.0, 16
PAST_LEN = 16384
POOL_WINDOWS = (2, 4, 8, 16)
POOL_GROUP_DIM = 256
POOL_HALO = 15
N_EXPERTS, TOP_K, N_EXPERT_GROUPS, TOPK_GROUPS = 256, 8, 8, 4
EXPERTS_PER_GROUP = N_EXPERTS // N_EXPERT_GROUPS
EXPERT_DIM, SHARED_DIM = 256, 256
ROUTED_SCALE = 2.5
LN_EPS = 1e-5
DEPTH = 1
ALPHA = (2.0 * DEPTH) ** 0.25
Q0, K0, V0, U0, GA0, GB0, IN_WIDTH = 0, 1024, 1280, 1536, 2560, 3584, 4608

LANES = 128
VMEM_LIMIT_BYTES = 56 * 1024 * 1024

TQ = 256
HALO_ROWS = 16
SB = 16
TB = 128
ROW_TILE = 8
CH = 256
CH_TILES = CH // ROW_TILE
CH_SHIFT = CH_TILES.bit_length() - 1
N_XBUF = 4
ZROWS = 256
TAIL_ROWS = -(-(N_EXPERTS * (ROW_TILE - 1) + CH) // ZROWS) * ZROWS
NEG_INF = float("-inf")


def _cparams(n_axes=1):
    return pltpu.CompilerParams(dimension_semantics=("arbitrary",) * n_axes,
                                vmem_limit_bytes=VMEM_LIMIT_BYTES)


def _layer_norm(r, g, b):
    mu = jnp.mean(r, axis=-1, keepdims=True)
    xc = r - mu
    var = jnp.mean(xc * xc, axis=-1, keepdims=True)
    return xc * lax.rsqrt(var + LN_EPS) * g + b


def _silu(x):
    return x * jax.nn.sigmoid(x)


def _pack_bf16_pairs(x):
    c = x.shape[1] // 2
    bits = lax.bitcast_convert_type(x.astype(BF16).astype(F32), U32)
    return (bits[:, :c] >> 16) | (bits[:, c:] & jnp.uint32(0xFFFF0000))


def _unpack_bf16_pairs(w):
    lo = lax.bitcast_convert_type(w << 16, F32)
    hi = lax.bitcast_convert_type(w & jnp.uint32(0xFFFF0000), F32)
    return lo, hi


def _rope128(c, ra, rb, rc):
    return c * ra + pltpu.roll(c, LANES - 8, 1) * rb + pltpu.roll(c, 8, 1) * rc


def _head_variants(x2):
    lo = lax.broadcasted_iota(I32, (1, LANES), 1) < HEAD_DIM
    sw = pltpu.roll(x2, HEAD_DIM, 1)
    zero = jnp.zeros_like(x2)
    return (jnp.where(lo, x2, zero).astype(BF16), jnp.where(lo, zero, sw).astype(BF16),
            jnp.where(lo, sw, zero).astype(BF16), jnp.where(lo, zero, x2).astype(BF16))


def _softmax_sink(s, extra, sink):
    m = jnp.maximum(jnp.max(s, axis=1, keepdims=True), sink)
    if extra is not None:
        m = jnp.maximum(m, extra)
    p = jnp.exp(s - m)
    den = jnp.sum(p, axis=1, keepdims=True) + jnp.exp(sink - m)
    if extra is None:
        return p * (1.0 / den), None
    pe = jnp.exp(extra - m)
    r = 1.0 / (den + pe)
    return p * r, pe * r


def _ada_kernel(c_ref, w_ref, b_ref, o_ref):
    s = _silu(c_ref[...]).astype(BF16)
    o_ref[...] = jnp.dot(s, w_ref[...].astype(BF16), preferred_element_type=F32) + b_ref[...]


def _ada(c_all, w_ada, b_ada):
    rows, n = c_all.shape[0], w_ada.shape[1]
    tn = 1536
    return pl.pallas_call(
        _ada_kernel,
        out_shape=jax.ShapeDtypeStruct((rows, n), F32),
        grid=(n // tn,),
        in_specs=[pl.BlockSpec((rows, D_MODEL), lambda j: (0, 0)),
                  pl.BlockSpec((D_MODEL, tn), lambda j: (0, j)),
                  pl.BlockSpec((1, tn), lambda j: (0, j))],
        out_specs=pl.BlockSpec((rows, tn), lambda j: (0, j)),
        compiler_params=_cparams(),
        name="ada",
    )(c_all, w_ada, b_ada.reshape(1, n))


def _mixer_prompt_kernel(sink_ref, x_ref, mods_ref, ra_ref, rb_ref, rc_ref, win_ref, wpool_ref,
                         pscale_ref, wout_ref, g1_ref, b1_ref,
                         x1_ref, h2_ref, kwin_ref, vwin_ref, pst_ref,
                         q_s, kk_s, vv_s, u_s, attn_s, pool_s):
    i = pl.program_id(0)
    last = pl.num_programs(0) - 1

    @pl.when(i == 0)
    def _():
        kk_s[:, :, 0:WINDOW, :] = jnp.zeros((N_KV_HEADS, 2, WINDOW, LANES), BF16)
        vv_s[:, :, 0:WINDOW, :] = jnp.zeros((N_KV_HEADS, 2, WINDOW, LANES), BF16)
        u_s[0:HALO_ROWS, :] = jnp.zeros((HALO_ROWS, D_MODEL), F32)

    shift1 = mods_ref[:, 0:1024]
    scale1 = mods_ref[:, 1024:2048]
    gate1 = mods_ref[:, 2048:3072]
    shift2 = mods_ref[:, 3072:4096]
    scale2 = mods_ref[:, 4096:5120]

    x = x_ref[...]
    h = (x * (1.0 + scale1) + shift1).astype(BF16)
    ra, rb, rc = ra_ref[...], rb_ref[...], rc_ref[...]

    q = jnp.dot(h, win_ref[:, Q0:K0], preferred_element_type=F32)
    for c in range(D_MODEL // LANES):
        qc = _rope128(q[:, c * LANES:(c + 1) * LANES], ra, rb, rc)
        q_s[:, c * LANES:(c + 1) * LANES] = (qc * (HEAD_DIM ** -0.5)).astype(BF16)

    k = jnp.dot(h, win_ref[:, K0:V0], preferred_element_type=F32)
    v = jnp.dot(h, win_ref[:, V0:U0], preferred_element_type=F32)
    for c in range(KV_WIDTH // LANES):
        kr = _rope128(k[:, c * LANES:(c + 1) * LANES], ra, rb, rc)
        vr = v[:, c * LANES:(c + 1) * LANES]

        @pl.when(i == last)
        def _(kr=kr, vr=vr, c=c):
            kwin_ref[:, c * LANES:(c + 1) * LANES] = kr[TQ - WINDOW:, :]
            vwin_ref[:, c * LANES:(c + 1) * LANES] = vr[TQ - WINDOW:, :]

        ka, kb, kc_, kd = _head_variants(kr)
        va, vb, vc_, vd = _head_variants(vr)
        kk_s[2 * c, 0, WINDOW:, :] = ka
        kk_s[2 * c, 1, WINDOW:, :] = kb
        kk_s[2 * c + 1, 0, WINDOW:, :] = kc_
        kk_s[2 * c + 1, 1, WINDOW:, :] = kd
        vv_s[2 * c, 0, WINDOW:, :] = va
        vv_s[2 * c, 1, WINDOW:, :] = vb
        vv_s[2 * c + 1, 0, WINDOW:, :] = vc_
        vv_s[2 * c + 1, 1, WINDOW:, :] = vd

    u_s[HALO_ROWS:, :] = jnp.dot(h, win_ref[:, U0:GA0], preferred_element_type=F32)

    @pl.when(i == last)
    def _():
        pst_ref[...] = u_s[HALO_ROWS + TQ - POOL_HALO:, :]

    rowpos = i * TQ + lax.broadcasted_iota(I32, (TQ, 1), 0)
    for g, w in enumerate(POOL_WINDOWS):
        cs = slice(g * POOL_GROUP_DIM, (g + 1) * POOL_GROUP_DIM)
        acc = u_s[HALO_ROWS:, cs]
        for m in range(1, w):
            acc = acc + u_s[HALO_ROWS - m:HALO_ROWS - m + TQ, cs]
        cnt = jnp.minimum(w, rowpos + 1).astype(F32)
        d = acc / cnt - u_s[HALO_ROWS:, cs]
        y = jnp.dot(d.astype(BF16), wpool_ref[g], preferred_element_type=F32)
        pool_s[:, cs] = y * pscale_ref[:, cs]

    a_idx = lax.broadcasted_iota(I32, (WINDOW, 4 * WINDOW), 0)
    j_idx = lax.broadcasted_iota(I32, (WINDOW, 4 * WINDOW), 1) & (2 * WINDOW - 1)
    no_prev = jnp.where(i > 0, 0, 4 * WINDOW)
    lo_lanes = lax.broadcasted_iota(I32, (1, LANES), 1) < HEAD_DIM
    key_half = lax.broadcasted_iota(I32, (4 * WINDOW, LANES), 0) >> 8
    lane_half = lax.broadcasted_iota(I32, (4 * WINDOW, LANES), 1) >> 6
    head_ones = (key_half == lane_half).astype(BF16)
    for qb in range(TQ // WINDOW):
        rows = slice(qb * WINDOW, (qb + 1) * WINDOW)
        band = slice(qb * WINDOW, (qb + 2) * WINDOW)
        a_prev = a_idx + no_prev if qb == 0 else a_idx
        mask = jnp.logical_or(jnp.logical_and(j_idx < WINDOW, j_idx >= a_prev),
                              jnp.logical_and(j_idx >= WINDOW, (j_idx - WINDOW) <= a_idx))
        for g in range(N_KV_HEADS):
            k2 = jnp.concatenate([kk_s[g, 0, band, :], kk_s[g, 1, band, :]], axis=0)
            v2 = jnp.concatenate([vv_s[g, 0, band, :], vv_s[g, 1, band, :]], axis=0)
            v2_ones = jnp.concatenate([v2, head_ones], axis=1)
            for p in range(2):
                c = 2 * g + p
                q2 = q_s[rows, c * LANES:(c + 1) * LANES]
                s2 = lax.dot_general(q2, k2, (((1,), (1,)), ((), ())), preferred_element_type=F32)
                s2 = jnp.where(mask, s2, NEG_INF)
                sink_a, sink_b = sink_ref[2 * c], sink_ref[2 * c + 1]
                ma = jnp.maximum(jnp.max(s2[:, :2 * WINDOW], axis=1, keepdims=True), sink_a)
                mb = jnp.maximum(jnp.max(s2[:, 2 * WINDOW:], axis=1, keepdims=True), sink_b)
                p2 = jnp.concatenate([jnp.exp(s2[:, :2 * WINDOW] - ma), jnp.exp(s2[:, 2 * WINDOW:] - mb)],
                                     axis=1).astype(BF16)
                o_den = jnp.dot(p2, v2_ones, preferred_element_type=F32)
                den = o_den[:, LANES:] + jnp.where(lo_lanes, jnp.exp(sink_a - ma), jnp.exp(sink_b - mb))
                attn_s[rows, c * LANES:(c + 1) * LANES] = o_den[:, :LANES] * (1.0 / den)

    kk_s[:, :, 0:WINDOW, :] = kk_s[:, :, TQ:, :]
    vv_s[:, :, 0:WINDOW, :] = vv_s[:, :, TQ:, :]
    u_s[0:HALO_ROWS, :] = u_s[TQ:, :]

    ga = jnp.dot(h, win_ref[:, GA0:GB0], preferred_element_type=F32)
    gb = jnp.dot(h, win_ref[:, GB0:IN_WIDTH], preferred_element_type=F32)
    merged = jax.nn.sigmoid(ga) * attn_s[...] + jax.nn.sigmoid(gb) * pool_s[...]
    y = jnp.dot(merged.astype(BF16), wout_ref[...], preferred_element_type=F32)
    x1 = _layer_norm(ALPHA * x + gate1 * y, g1_ref[...], b1_ref[...])
    x1_ref[...] = x1
    h2_ref[...] = _pack_bf16_pairs(x1 * (1.0 + scale2) + shift2)


def _mixer_prompt(x, mods_p, tabs, w_in_b, sinks, w_pool_b, pool_scale, w_out_b, ln1_g, ln1_b):
    t = x.shape[0]
    full = lambda shape: pl.BlockSpec(shape, lambda i: (0,) * len(shape))
    rowblk = lambda width: pl.BlockSpec((TQ, width), lambda i: (i, 0))
    return pl.pallas_call(
        _mixer_prompt_kernel,
        out_shape=(jax.ShapeDtypeStruct((t, D_MODEL), F32),
                   jax.ShapeDtypeStruct((t, D_MODEL // 2), U32),
                   jax.ShapeDtypeStruct((WINDOW, KV_WIDTH), F32),
                   jax.ShapeDtypeStruct((WINDOW, KV_WIDTH), F32),
                   jax.ShapeDtypeStruct((POOL_HALO, D_MODEL), F32)),
        grid=(t // TQ,),
        in_specs=[pl.BlockSpec(memory_space=pltpu.SMEM),
                  rowblk(D_MODEL), full((1, 6 * D_MODEL)),
                  rowblk(LANES), rowblk(LANES), rowblk(LANES),
                  full((D_MODEL, IN_WIDTH)), full((4, POOL_GROUP_DIM, POOL_GROUP_DIM)),
                  full((1, D_MODEL)), full((D_MODEL, D_MODEL)), full((1, D_MODEL)), full((1, D_MODEL))],
        out_specs=(rowblk(D_MODEL), rowblk(D_MODEL // 2),
                   full((WINDOW, KV_WIDTH)), full((WINDOW, KV_WIDTH)), full((POOL_HALO, D_MODEL))),
        scratch_shapes=[pltpu.VMEM((TQ, D_MODEL), BF16),
                        pltpu.VMEM((N_KV_HEADS, 2, WINDOW + TQ, LANES), BF16),
                        pltpu.VMEM((N_KV_HEADS, 2, WINDOW + TQ, LANES), BF16),
                        pltpu.VMEM((HALO_ROWS + TQ, D_MODEL), F32),
                        pltpu.VMEM((TQ, D_MODEL), F32),
                        pltpu.VMEM((TQ, D_MODEL), F32)],
        compiler_params=_cparams(),
        name="mixer_prompt",
    )(sinks, x, mods_p, *tabs, w_in_b, w_pool_b, pool_scale, w_out_b, ln1_g, ln1_b)


def _sample_proj_kernel(x_ref, mods_ref, ra_ref, rb_ref, rc_ref, win_ref, spt_ref, wpool_ref, pscale_ref,
                        q_ref, k_ref, v_ref, u_ref, pool_ref, ga_ref, gb_ref):
    x = x_ref[...]
    h = (x * (1.0 + mods_ref[:, 1024:2048]) + mods_ref[:, 0:1024]).astype(BF16)
    ra, rb, rc = ra_ref[...], rb_ref[...], rc_ref[...]
    q = jnp.dot(h, win_ref[:, Q0:K0], preferred_element_type=F32)
    for c in range(D_MODEL // LANES):
        qc = _rope128(q[:, c * LANES:(c + 1) * LANES], ra, rb, rc)
        q_ref[:, c * LANES:(c + 1) * LANES] = (qc * (HEAD_DIM ** -0.5)).astype(BF16)
    k = jnp.dot(h, win_ref[:, K0:V0], preferred_element_type=F32)
    for c in range(KV_WIDTH // LANES):
        k_ref[:, c * LANES:(c + 1) * LANES] = _rope128(k[:, c * LANES:(c + 1) * LANES], ra, rb, rc)
    v_ref[...] = jnp.dot(h, win_ref[:, V0:U0], preferred_element_type=F32)
    u = jnp.dot(h, win_ref[:, U0:GA0], preferred_element_type=F32)
    u_ref[...] = u
    for g, w in enumerate(POOL_WINDOWS):
        cs = slice(g * POOL_GROUP_DIM, (g + 1) * POOL_GROUP_DIM)
        acc = u[:, cs]
        for m in range(1, w):
            acc = acc + spt_ref[POOL_HALO - m, :, cs]
        d = acc / float(min(w, PAST_LEN + 1)) - u[:, cs]
        y = jnp.dot(d.astype(BF16), wpool_ref[g], preferred_element_type=F32)
        pool_ref[:, cs] = y * pscale_ref[:, cs]
    ga_ref[...] = jnp.dot(h, win_ref[:, GA0:GB0], preferred_element_type=F32)
    gb_ref[...] = jnp.dot(h, win_ref[:, GB0:IN_WIDTH], preferred_element_type=F32)


def _sample_proj(x_s, mods_s, tabs, w_in_b, sp_t, w_pool_b, pool_scale):
    b = x_s.shape[0]
    sd = lambda w, dt=F32: jax.ShapeDtypeStruct((b, w), dt)
    return pl.pallas_call(
        _sample_proj_kernel,
        out_shape=(sd(D_MODEL, BF16), sd(KV_WIDTH), sd(KV_WIDTH), sd(D_MODEL), sd(D_MODEL),
                   sd(D_MODEL), sd(D_MODEL)),
        compiler_params=pltpu.CompilerParams(vmem_limit_bytes=VMEM_LIMIT_BYTES),
        name="sample_proj",
    )(x_s, mods_s, *tabs, w_in_b, sp_t, w_pool_b, pool_scale)


def _sample_attn_kernel(sink_ref, q_ref, ck_ref, cv_ref, kn_ref, vn_ref, o_ref):
    nkeys = SB * WINDOW
    lo = lax.broadcasted_iota(I32, (1, LANES), 1) < HEAD_DIM
    row_b = lax.broadcasted_iota(I32, (2 * SB, 2 * nkeys), 0) & (SB - 1)
    col_b = (lax.broadcasted_iota(I32, (2 * SB, 2 * nkeys), 1) & (nkeys - 1)) >> 7
    own = row_b == col_b
    first_pair = lax.broadcasted_iota(I32, (2 * SB, 1), 0) < SB
    for c in range(KV_WIDTH // LANES):
        kvar = _head_variants(ck_ref[:, c * LANES:(c + 1) * LANES])
        vvar = _head_variants(cv_ref[:, c * LANES:(c + 1) * LANES])
        kn = kn_ref[:, c * LANES:(c + 1) * LANES].astype(BF16).astype(F32)
        vn = vn_ref[:, c * LANES:(c + 1) * LANES].astype(BF16).astype(F32)
        kn_sw, vn_sw = pltpu.roll(kn, HEAD_DIM, 1), pltpu.roll(vn, HEAD_DIM, 1)
        for e in range(2):
            g = 2 * c + e
            k2 = jnp.concatenate([kvar[2 * e], kvar[2 * e + 1]], axis=0)
            v2 = jnp.concatenate([vvar[2 * e], vvar[2 * e + 1]], axis=0)
            kdup = jnp.where(lo, kn, kn_sw) if e == 0 else jnp.where(lo, kn_sw, kn)
            vdup = jnp.where(lo, vn, vn_sw) if e == 0 else jnp.where(lo, vn_sw, vn)
            qa = q_ref[:, (2 * g) * LANES:(2 * g + 1) * LANES]
            qb = q_ref[:, (2 * g + 1) * LANES:(2 * g + 2) * LANES]
            q4 = jnp.concatenate([qa, qb], axis=0)
            s = lax.dot_general(q4, k2, (((1,), (1,)), ((), ())), preferred_element_type=F32)
            s = jnp.where(own, s, NEG_INF)
            prod = q4.astype(F32) * jnp.concatenate([kdup, kdup], axis=0)
            s_new = (jnp.sum(jnp.where(lo, prod, 0.0), axis=1, keepdims=True),
                     jnp.sum(jnp.where(lo, 0.0, prod), axis=1, keepdims=True))
            ps, pn = [], []
            for hh in range(2):
                sink = jnp.where(first_pair, sink_ref[4 * g + hh], sink_ref[4 * g + 2 + hh])
                p_c, p_n = _softmax_sink(s[:, hh * nkeys:(hh + 1) * nkeys], s_new[hh], sink)
                ps.append(p_c)
                pn.append(p_n)
            p2 = jnp.concatenate(ps, axis=1).astype(BF16)
            o = jnp.dot(p2, v2, preferred_element_type=F32)
            o = o + jnp.where(lo, pn[0], pn[1]) * jnp.concatenate([vdup, vdup], axis=0)
            o_ref[:, (2 * g) * LANES:(2 * g + 1) * LANES] = o[:SB]
            o_ref[:, (2 * g + 1) * LANES:(2 * g + 2) * LANES] = o[SB:]


def _sample_attn(sinks, q, ck2d, cv2d, k_new, v_new):
    b = q.shape[0]
    return pl.pallas_call(
        _sample_attn_kernel,
        out_shape=jax.ShapeDtypeStruct((b, D_MODEL), F32),
        grid=(b // SB,),
        in_specs=[pl.BlockSpec(memory_space=pltpu.SMEM),
                  pl.BlockSpec((SB, D_MODEL), lambda i: (i, 0)),
                  pl.BlockSpec((SB * WINDOW, KV_WIDTH), lambda i: (i, 0)),
                  pl.BlockSpec((SB * WINDOW, KV_WIDTH), lambda i: (i, 0)),
                  pl.BlockSpec((SB, KV_WIDTH), lambda i: (i, 0)),
                  pl.BlockSpec((SB, KV_WIDTH), lambda i: (i, 0))],
        out_specs=pl.BlockSpec((SB, D_MODEL), lambda i: (i, 0)),
        compiler_params=_cparams(),
        name="sample_attn",
    )(sinks, q, ck2d, cv2d, k_new, v_new)


def _sample_merge_kernel(x_ref, mods_ref, attn_ref, pool_ref, ga_ref, gb_ref, wout_ref, g1_ref, b1_ref,
                         x1_ref, h2_ref):
    merged = jax.nn.sigmoid(ga_ref[...]) * attn_ref[...] + jax.nn.sigmoid(gb_ref[...]) * pool_ref[...]
    y = jnp.dot(merged.astype(BF16), wout_ref[...], preferred_element_type=F32)
    x1 = _layer_norm(ALPHA * x_ref[...] + mods_ref[:, 2048:3072] * y, g1_ref[...], b1_ref[...])
    x1_ref[...] = x1
    h2_ref[...] = _pack_bf16_pairs(x1 * (1.0 + mods_ref[:, 4096:5120]) + mods_ref[:, 3072:4096])


def _sample_merge(x_s, mods_s, attn, pool, ga, gb, w_out_b, ln1_g, ln1_b):
    b = x_s.shape[0]
    return pl.pallas_call(
        _sample_merge_kernel,
        out_shape=(jax.ShapeDtypeStruct((b, D_MODEL), F32), jax.ShapeDtypeStruct((b, D_MODEL // 2), U32)),
        compiler_params=pltpu.CompilerParams(vmem_limit_bytes=VMEM_LIMIT_BYTES),
        name="sample_merge",
    )(x_s, mods_s, attn, pool, ga, gb, w_out_b, ln1_g, ln1_b)


def _router_kernel(h2_ref, wrt_ref, bias_ref, idx_ref, wts_ref, rank_ref, cnt_ref, carry_s):
    i = pl.program_id(0)

    @pl.when(i == 0)
    def _():
        carry_s[...] = jnp.zeros_like(carry_s)

    lo, hi = _unpack_bf16_pairs(h2_ref[...])
    nt = (((1,), (1,)), ((), ()))
    half = D_MODEL // 2
    logits = (lax.dot_general(wrt_ref[:, :half], lo.astype(BF16), nt, preferred_element_type=F32)
              + lax.dot_general(wrt_ref[:, half:], hi.astype(BF16), nt, preferred_element_type=F32))
    scores = jax.nn.sigmoid(logits)
    sel = scores + bias_ref[...]

    gi = lax.broadcasted_iota(I32, (EXPERTS_PER_GROUP, TB), 0).astype(F32)
    gscore = []
    for g in range(N_EXPERT_GROUPS):
        blk = sel[g * EXPERTS_PER_GROUP:(g + 1) * EXPERTS_PER_GROUP, :]
        m1 = jnp.max(blk, axis=0, keepdims=True)
        f1 = jnp.min(jnp.where(blk == m1, gi, float(EXPERTS_PER_GROUP)), axis=0, keepdims=True)
        m2 = jnp.max(jnp.where(gi == f1, NEG_INF, blk), axis=0, keepdims=True)
        gscore.append(m1 + m2)

    gsel = [jnp.zeros((1, TB), jnp.bool_) for _ in range(N_EXPERT_GROUPS)]
    for _ in range(TOPK_GROUPS):
        m = gscore[0]
        for g in range(1, N_EXPERT_GROUPS):
            m = jnp.maximum(m, gscore[g])
        found = jnp.zeros((1, TB), jnp.bool_)
        for g in range(N_EXPERT_GROUPS):
            pick = jnp.logical_and(gscore[g] == m, jnp.logical_not(found))
            found = jnp.logical_or(found, pick)
            gsel[g] = jnp.logical_or(gsel[g], pick)
            gscore[g] = jnp.where(pick, NEG_INF, gscore[g])
    cur = jnp.concatenate(
        [jnp.where(gsel[g], sel[g * EXPERTS_PER_GROUP:(g + 1) * EXPERTS_PER_GROUP, :], NEG_INF)
         for g in range(N_EXPERT_GROUPS)], axis=0)

    ei = lax.broadcasted_iota(I32, (N_EXPERTS, TB), 0).astype(F32)
    onehot = jnp.zeros((N_EXPERTS, TB), F32)
    picks, raw_w = [], []
    for _ in range(TOP_K):
        m = jnp.max(cur, axis=0, keepdims=True)
        f = jnp.min(jnp.where(cur == m, ei, float(N_EXPERTS)), axis=0, keepdims=True)
        pick = ei == f
        picks.append(f)
        raw_w.append(jnp.sum(jnp.where(pick, scores, 0.0), axis=0, keepdims=True))
        cur = jnp.where(pick, NEG_INF, cur)
        onehot = onehot + pick.astype(F32)
    wsum = raw_w[0]
    for k in range(1, TOP_K):
        wsum = wsum + raw_w[k]

    tri = (lax.broadcasted_iota(I32, (TB, TB), 0) <= lax.broadcasted_iota(I32, (TB, TB), 1)).astype(BF16)
    incl = jnp.dot(onehot.astype(BF16), tri, preferred_element_type=F32)
    pos = carry_s[...] + incl - onehot
    for k in range(TOP_K):
        idx_ref[k:k + 1, :] = picks[k].astype(I32)
        wts_ref[k:k + 1, :] = raw_w[k] / wsum * ROUTED_SCALE
        rank_ref[k:k + 1, :] = jnp.sum(jnp.where(ei == picks[k], pos, 0.0), axis=0, keepdims=True).astype(I32)
    carry = carry_s[...] + jnp.sum(onehot, axis=1, keepdims=True)
    carry_s[...] = carry
    cnt_ref[...] = carry


def _router(h2_all, w_router_t_b, router_bias):
    t_all = h2_all.shape[0]
    tok = lambda dt: jax.ShapeDtypeStruct((TOP_K, t_all), dt)
    return pl.pallas_call(
        _router_kernel,
        out_shape=(tok(I32), tok(F32), tok(I32), jax.ShapeDtypeStruct((N_EXPERTS, LANES), F32)),
        grid=(t_all // TB,),
        in_specs=[pl.BlockSpec((TB, D_MODEL // 2), lambda i: (i, 0)),
                  pl.BlockSpec((N_EXPERTS, D_MODEL), lambda i: (0, 0)),
                  pl.BlockSpec((N_EXPERTS, 1), lambda i: (0, 0))],
        out_specs=(pl.BlockSpec((TOP_K, TB), lambda i: (0, i)),
                   pl.BlockSpec((TOP_K, TB), lambda i: (0, i)),
                   pl.BlockSpec((TOP_K, TB), lambda i: (0, i)),
                   pl.BlockSpec((N_EXPERTS, LANES), lambda i: (0, 0))),
        scratch_shapes=[pltpu.VMEM((N_EXPERTS, LANES), F32)],
        compiler_params=_cparams(),
        name="router",
    )(h2_all, w_router_t_b, router_bias.reshape(N_EXPERTS, 1))


def _moe_meta_kernel(idx_ref, rank_ref, cnt_ref, dest_ref, emeta_ref, tstart_s):
    i = pl.program_id(0)

    @pl.when(i == 0)
    def _():
        cnt_col = cnt_ref[:, 0:1]
        tiles_col = jnp.floor((cnt_col + float(ROW_TILE - 1)) * (1.0 / ROW_TILE))
        e_r = lax.broadcasted_iota(I32, (N_EXPERTS, N_EXPERTS), 0)
        e_c = lax.broadcasted_iota(I32, (N_EXPERTS, N_EXPERTS), 1)
        diag = e_r == e_c
        tiles_row = jnp.sum(jnp.where(diag, tiles_col, 0.0), axis=0, keepdims=True)
        cnt_row = jnp.sum(jnp.where(diag, cnt_col, 0.0), axis=0, keepdims=True)
        tstart_row = jnp.sum(jnp.where(e_r < e_c, tiles_col, 0.0), axis=0, keepdims=True)
        tstart_col = jnp.sum(jnp.where(e_c < e_r, tiles_row, 0.0), axis=1, keepdims=True)
        tstart_s[...] = jnp.broadcast_to(tstart_col, (N_EXPERTS, LANES))
        emeta_ref[0:1, :] = tstart_row.astype(I32)
        emeta_ref[1:2, :] = tiles_row.astype(I32)
        emeta_ref[2:3, :] = cnt_row.astype(I32)
        later_used = jnp.logical_and(e_r > e_c, tiles_col > 0.0)
        next_used = jnp.min(jnp.where(later_used, e_r.astype(F32), float(N_EXPERTS)), axis=0, keepdims=True)
        emeta_ref[3:4, :] = next_used.astype(I32)
        emeta_ref[4:8, :] = jnp.zeros((4, N_EXPERTS), I32)

    ei = lax.broadcasted_iota(I32, (N_EXPERTS, TB), 0)
    tstart = tstart_s[...]
    for k in range(TOP_K):
        st = jnp.sum(jnp.where(ei == idx_ref[k:k + 1, :], tstart, 0.0), axis=0, keepdims=True)
        dest_ref[k:k + 1, :] = st.astype(I32) * ROW_TILE + rank_ref[k:k + 1, :]


def _moe_meta(idx, rank, counts):
    t_all = idx.shape[1]
    return pl.pallas_call(
        _moe_meta_kernel,
        out_shape=(jax.ShapeDtypeStruct((TOP_K, t_all), I32), jax.ShapeDtypeStruct((8, N_EXPERTS), I32)),
        grid=(t_all // TB,),
        in_specs=[pl.BlockSpec((TOP_K, TB), lambda i: (0, i)),
                  pl.BlockSpec((TOP_K, TB), lambda i: (0, i)),
                  pl.BlockSpec((N_EXPERTS, LANES), lambda i: (0, 0))],
        out_specs=(pl.BlockSpec((TOP_K, TB), lambda i: (0, i)),
                   pl.BlockSpec((8, N_EXPERTS), lambda i: (0, 0))),
        scratch_shapes=[pltpu.VMEM((N_EXPERTS, LANES), F32)],
        compiler_params=_cparams(),
        name="moe_meta",
    )(idx, rank, counts)


def _zero_tail(zbuf, out_ref, zsem, base):
    zbuf[...] = jnp.zeros_like(zbuf)
    copies = [pltpu.make_async_copy(zbuf, out_ref.at[pl.ds(base + j * ZROWS, ZROWS), :], zsem)
              for j in range(TAIL_ROWS // ZROWS)]
    for c in copies:
        c.start()
    for c in copies:
        c.wait()


def _dispatch_kernel(emeta_ref, dest_ref, h2_ref, xs_ref, zbuf, sem, zsem, *, base):
    i = pl.program_id(0)

    @pl.when(i == 0)
    def _():
        _zero_tail(zbuf, xs_ref, zsem, base)

        def gap_rows(e, fn):
            cnt = emeta_ref[2 * N_EXPERTS + e]
            first = emeta_ref[e] * ROW_TILE + cnt
            n_gap = emeta_ref[N_EXPERTS + e] * ROW_TILE - cnt

            def body(r, carry):
                fn(pltpu.make_async_copy(zbuf.at[pl.ds(0, 1), :], xs_ref.at[pl.ds(first + r, 1), :], zsem))
                return carry

            lax.fori_loop(0, n_gap, body, 0)

        def start_gaps(e, carry):
            gap_rows(e, lambda c: c.start())
            return carry

        def wait_gaps(e, carry):
            gap_rows(e, lambda c: c.wait())
            return carry

        lax.fori_loop(0, N_EXPERTS, start_gaps, 0)
        lax.fori_loop(0, N_EXPERTS, wait_gaps, 0)

    def row_copy(a, b, k):
        d = dest_ref[0, 0, k * TB + a * ROW_TILE + b]
        return pltpu.make_async_copy(h2_ref.at[a, pl.ds(b, 1), :], xs_ref.at[pl.ds(d, 1), :], sem)

    def for_rows(fn):
        def body(a, carry):
            for b in range(ROW_TILE):
                for k in range(TOP_K):
                    fn(row_copy(a, b, k), k)
            return carry

        lax.fori_loop(0, TB // ROW_TILE, body, 0)

    for_rows(lambda c, k: c.start(priority=k % 2))
    for_rows(lambda c, k: c.wait())


def _dispatch(emeta_flat, dest_blk, h2_all):
    t_all = h2_all.shape[0]
    base = t_all * TOP_K
    grid_spec = pltpu.PrefetchScalarGridSpec(
        num_scalar_prefetch=1,
        grid=(t_all // TB,),
        in_specs=[pl.BlockSpec((1, 1, TOP_K * TB), lambda i, m: (i, 0, 0), memory_space=pltpu.SMEM),
                  pl.BlockSpec((TB // ROW_TILE, ROW_TILE, D_MODEL // 2), lambda i, m: (i, 0, 0))],
        out_specs=pl.BlockSpec(memory_space=pl.ANY),
        scratch_shapes=[pltpu.VMEM((ZROWS, D_MODEL // 2), U32),
                        pltpu.SemaphoreType.DMA(()), pltpu.SemaphoreType.DMA(())],
    )
    return pl.pallas_call(
        functools.partial(_dispatch_kernel, base=base),
        out_shape=jax.ShapeDtypeStruct((base + TAIL_ROWS, D_MODEL // 2), U32),
        grid_spec=grid_spec,
        compiler_params=_cparams(),
        name="dispatch",
    )(emeta_flat, dest_blk, h2_all.reshape(t_all // ROW_TILE, ROW_TILE, D_MODEL // 2))


def _experts_kernel(emeta_ref, xs_ref, wg_ref, wu_ref, wd_ref, ys_ref,
                    xbuf, ybuf, zbuf, wgu_s, wd_s, st, pend_row, pend_n, isem, osem, zsem, *, base):
    e = pl.program_id(0)
    row0 = emeta_ref[e] * ROW_TILE
    n_tiles = emeta_ref[N_EXPERTS + e]
    n_chunks = (n_tiles + (CH_TILES - 1)) >> CH_SHIFT

    def load(row, slot):
        return pltpu.make_async_copy(xs_ref.at[pl.ds(pl.multiple_of(row, ROW_TILE), CH), :],
                                     xbuf.at[slot], isem.at[slot])

    def request_next(slot):
        ce, cj = st[1], st[2]

        @pl.when(ce < N_EXPERTS)
        def _():
            load(emeta_ref[ce] * ROW_TILE + cj * CH, slot).start()
            c_chunks = (emeta_ref[N_EXPERTS + ce] + (CH_TILES - 1)) >> CH_SHIFT
            more = cj + 1 < c_chunks
            st[1] = jnp.where(more, ce, emeta_ref[3 * N_EXPERTS + ce])
            st[2] = jnp.where(more, cj + 1, 0)

    def out_copies(slot, fn):
        row, n = pend_row[slot], pend_n[slot]

        @pl.when(n == CH_TILES + 1)
        def _():
            fn(pltpu.make_async_copy(ybuf.at[slot], ys_ref.at[pl.ds(pl.multiple_of(row, ROW_TILE), CH), :],
                                     osem.at[slot]))

        def body(t, carry):
            fn(pltpu.make_async_copy(
                ybuf.at[slot, pl.ds(pl.multiple_of(t * ROW_TILE, ROW_TILE), ROW_TILE), :],
                ys_ref.at[pl.ds(pl.multiple_of(row + t * ROW_TILE, ROW_TILE), ROW_TILE), :], osem.at[slot]))
            return carry

        lax.fori_loop(0, jnp.where(n <= CH_TILES, n, 0), body, 0)

    @pl.when(e == 0)
    def _():
        _zero_tail(zbuf, ys_ref, zsem, base)
        st[0] = 0
        st[1] = jnp.where(n_tiles > 0, 0, emeta_ref[3 * N_EXPERTS])
        st[2] = 0
        pend_n[0] = 0
        pend_n[1] = 0
        for s in range(N_XBUF - 1):
            request_next(s)

    @pl.when(n_tiles > 0)
    def _():
        wgu_s[:, :EXPERT_DIM] = wg_ref[...].astype(BF16)
        wgu_s[:, EXPERT_DIM:] = wu_ref[...].astype(BF16)
        wd_s[...] = wd_ref[...].astype(BF16)
        g0 = st[0]

        def chunk(j, carry):
            g = g0 + j
            slot = g & (N_XBUF - 1)
            load(row0 + j * CH, slot).wait()
            request_next((g + N_XBUF - 1) & (N_XBUF - 1))

            lo, hi = _unpack_bf16_pairs(xbuf[slot])
            half = D_MODEL // 2
            gu = (jnp.dot(lo.astype(BF16), wgu_s[:half, :], preferred_element_type=F32)
                  + jnp.dot(hi.astype(BF16), wgu_s[half:, :], preferred_element_type=F32))
            act = _silu(gu[:, :EXPERT_DIM]) * gu[:, EXPERT_DIM:]
            y = _pack_bf16_pairs(jnp.dot(act.astype(BF16), wd_s[...], preferred_element_type=F32))

            oslot = g & 1
            out_copies(oslot, lambda c: c.wait())
            ybuf[oslot] = y
            pend_row[oslot] = row0 + j * CH
            pend_n[oslot] = jnp.where((j + 1) * CH_TILES <= n_tiles, CH_TILES + 1, n_tiles - j * CH_TILES)
            out_copies(oslot, lambda c: c.start())
            return carry

        lax.fori_loop(0, n_chunks, chunk, 0)
        st[0] = g0 + n_chunks

    @pl.when(e == pl.num_programs(0) - 1)
    def _():
        for s in range(2):
            out_copies(s, lambda c: c.wait())
            pend_n[s] = 0


def _experts(emeta_flat, xs, w_gate, w_up, w_down):
    base = xs.shape[0] - TAIL_ROWS
    w_map = lambda e, m: (e, 0, 0)
    grid_spec = pltpu.PrefetchScalarGridSpec(
        num_scalar_prefetch=1,
        grid=(N_EXPERTS,),
        in_specs=[pl.BlockSpec(memory_space=pl.ANY),
                  pl.BlockSpec((None, D_MODEL, EXPERT_DIM), w_map),
                  pl.BlockSpec((None, D_MODEL, EXPERT_DIM), w_map),
                  pl.BlockSpec((None, EXPERT_DIM, D_MODEL), w_map)],
        out_specs=pl.BlockSpec(memory_space=pl.ANY),
        scratch_shapes=[pltpu.VMEM((N_XBUF, CH, D_MODEL // 2), U32),
                        pltpu.VMEM((2, CH, D_MODEL // 2), U32),
                        pltpu.VMEM((ZROWS, D_MODEL // 2), U32),
                        pltpu.VMEM((D_MODEL, 2 * EXPERT_DIM), BF16),
                        pltpu.VMEM((EXPERT_DIM, D_MODEL), BF16),
                        pltpu.SMEM((3,), I32), pltpu.SMEM((2,), I32), pltpu.SMEM((2,), I32),
                        pltpu.SemaphoreType.DMA((N_XBUF,)), pltpu.SemaphoreType.DMA((2,)),
                        pltpu.SemaphoreType.DMA(())],
    )
    return pl.pallas_call(
        functools.partial(_experts_kernel, base=base),
        out_shape=jax.ShapeDtypeStruct(xs.shape, U32),
        grid_spec=grid_spec,
        compiler_params=_cparams(),
        name="experts",
    )(emeta_flat, xs, w_gate, w_up, w_down)


def _combine_kernel(dest_ref, dnext_ref, x1_ref, mods_ref, wts_ref, ys_ref, wsgu_ref, wsd_ref, g2_ref, b2_ref,
                    out_ref, buf, sem):
    i = pl.program_id(0)
    slot = i & 1

    def row_copy(dref, s, a, b, k):
        return pltpu.make_async_copy(ys_ref.at[pl.ds(dref[0, 0, k * TB + a * ROW_TILE + b], 1), :],
                                     buf.at[s, k, a, pl.ds(b, 1), :], sem.at[s])

    def for_rows(fn):
        def body(a, carry):
            for b in range(ROW_TILE):
                for k in range(TOP_K):
                    fn(a, b, k)
            return carry

        lax.fori_loop(0, TB // ROW_TILE, body, 0)

    def gather(dref, s):
        for a in range(TB // ROW_TILE):
            for b in range(ROW_TILE):
                for k in range(TOP_K):
                    row_copy(dref, s, a, b, k).start(priority=k % 2)

    @pl.when(i == 0)
    def _():
        gather(dest_ref, 0)

    @pl.when(i + 1 < pl.num_programs(0))
    def _():
        for s in range(2):
            @pl.when(slot == 1 - s)
            def _(s=s):
                gather(dnext_ref, s)

    x1 = x1_ref[...]
    h2 = (x1 * (1.0 + mods_ref[:, 4096:5120]) + mods_ref[:, 3072:4096]).astype(BF16)
    gu = jnp.dot(h2, wsgu_ref[...], preferred_element_type=F32)
    act = _silu(gu[:, :SHARED_DIM]) * gu[:, SHARED_DIM:]
    shared = jnp.dot(act.astype(BF16), wsd_ref[...], preferred_element_type=F32)

    for_rows(lambda a, b, k: row_copy(dest_ref, slot, a, b, k).wait())

    acc_lo = jnp.zeros((TB, D_MODEL // 2), F32)
    acc_hi = jnp.zeros((TB, D_MODEL // 2), F32)
    for k in range(TOP_K):
        lo, hi = _unpack_bf16_pairs(buf[slot, k].reshape(TB, D_MODEL // 2))
        wk = wts_ref[:, k:k + 1]
        acc_lo = acc_lo + wk * lo
        acc_hi = acc_hi + wk * hi
    m = shared + jnp.concatenate([acc_lo, acc_hi], axis=1)
    out_ref[...] = _layer_norm(ALPHA * x1 + mods_ref[:, 5120:6144] * m, g2_ref[...], b2_ref[...])


def _combine(dest_blk, x1, mods, wts_t, ys, wsgu_b, wsd_b, ln2_g, ln2_b, blk_off):
    t = x1.shape[0]
    per_row = mods.shape[0] != 1
    mods_spec = (pl.BlockSpec((TB, 6 * D_MODEL), lambda i: (i, 0)) if per_row
                 else pl.BlockSpec((1, 6 * D_MODEL), lambda i: (0, 0)))
    full = lambda shape: pl.BlockSpec(shape, lambda i: (0,) * len(shape))
    n_steps = t // TB
    return pl.pallas_call(
        _combine_kernel,
        out_shape=jax.ShapeDtypeStruct((t, D_MODEL), F32),
        grid=(n_steps,),
        in_specs=[pl.BlockSpec((1, 1, TOP_K * TB), lambda i: (i + blk_off, 0, 0), memory_space=pltpu.SMEM),
                  pl.BlockSpec((1, 1, TOP_K * TB), lambda i: (jnp.minimum(i + 1, n_steps - 1) + blk_off, 0, 0),
                               memory_space=pltpu.SMEM),
                  pl.BlockSpec((TB, D_MODEL), lambda i: (i, 0)),
                  mods_spec,
                  pl.BlockSpec((TB, TOP_K), lambda i: (i + blk_off, 0)),
                  pl.BlockSpec(memory_space=pl.ANY),
                  full((D_MODEL, 2 * SHARED_DIM)), full((SHARED_DIM, D_MODEL)),
                  full((1, D_MODEL)), full((1, D_MODEL))],
        out_specs=pl.BlockSpec((TB, D_MODEL), lambda i: (i, 0)),
        scratch_shapes=[pltpu.VMEM((2, TOP_K, TB // ROW_TILE, ROW_TILE, D_MODEL // 2), U32),
                        pltpu.SemaphoreType.DMA((2,))],
        compiler_params=_cparams(),
        name="combine",
    )(dest_blk, dest_blk, x1, mods, wts_t, ys, wsgu_b, wsd_b, ln2_g, ln2_b)


def _rope_tables(pos):
    half = ROT_DIM // 2
    inv = ROPE_THETA ** (-jnp.arange(0, ROT_DIM, 2, dtype=F32) / ROT_DIM)
    ang = pos.astype(F32)[:, None] * inv[None, :]
    cos, sin = jnp.cos(ang), jnp.sin(ang)
    n = pos.shape[0]
    ones = jnp.ones((n, HEAD_DIM - ROT_DIM), F32)
    z = lambda w: jnp.zeros((n, w), F32)
    ra = jnp.concatenate([cos, cos, ones], axis=1)
    rb = jnp.concatenate([-sin, z(HEAD_DIM - half)], axis=1)
    rc = jnp.concatenate([z(half), sin, z(HEAD_DIM - ROT_DIM)], axis=1)
    return tuple(jnp.tile(t, (1, LANES // HEAD_DIM)) for t in (ra, rb, rc))


def kernel(x_prompt, x_sample, cache_k, cache_v, state_pool, c_prompt, c_sample, w_ada, b_ada, w_in, attn_sinks, w_pool, pool_scale, w_out, ln1_g, ln1_b, w_router, router_bias, w_exp_gate, w_exp_up, w_exp_down, w_sh_gate, w_sh_up, w_sh_down, ln2_g, ln2_b):
    assert w_ada.shape[0] == DEPTH == 1 and x_prompt.shape[0] == 1 and x_sample.shape[1] == 1
    t_p, n_s = x_prompt.shape[1], x_sample.shape[0]
    assert t_p % TQ == 0 and n_s % SB == 0 and n_s == TB and cache_k.shape[2] == WINDOW
    t_all = t_p + n_s

    row = lambda a: a[0].reshape(1, -1)
    w_in_b, w_out_b, w_pool_b = w_in[0].astype(BF16), w_out[0].astype(BF16), w_pool[0].astype(BF16)
    wsgu_b = jnp.concatenate([w_sh_gate[0], w_sh_up[0]], axis=1).astype(BF16)
    wsd_b = w_sh_down[0].astype(BF16)
    w_router_t_b = w_router[0].T.astype(BF16)
    sinks = attn_sinks[0]
    pscale, g1, b1, g2, b2 = row(pool_scale), row(ln1_g), row(ln1_b), row(ln2_g), row(ln2_b)

    x_s = x_sample.reshape(n_s, D_MODEL)
    c_all = jnp.concatenate([c_sample, c_prompt, jnp.zeros((7, D_MODEL), F32)], axis=0)
    mods = _ada(c_all, w_ada[0], b_ada[0])
    mods_s, mods_p = mods[:n_s], mods[n_s:n_s + 1]

    x_p = x_prompt.reshape(t_p, D_MODEL)
    x1_p, h2_p, k_win, v_win, pool_st = _mixer_prompt(
        x_p, mods_p, _rope_tables(jnp.arange(t_p)), w_in_b, sinks, w_pool_b, pscale, w_out_b, g1, b1)

    sp_t = jnp.transpose(state_pool[0], (1, 0, 2))
    q_s, k_new, v_new, u_new, pool_s, ga_s, gb_s = _sample_proj(
        x_s, mods_s, _rope_tables(jnp.full((1,), PAST_LEN)), w_in_b, sp_t, w_pool_b, pscale)
    attn_s = _sample_attn(sinks, q_s, cache_k[0].reshape(n_s * WINDOW, KV_WIDTH),
                          cache_v[0].reshape(n_s * WINDOW, KV_WIDTH), k_new, v_new)
    x1_s, h2_s = _sample_merge(x_s, mods_s, attn_s, pool_s, ga_s, gb_s, w_out_b, g1, b1)

    h2_all = jnp.concatenate([h2_p, h2_s], axis=0)
    idx, wts, rank, counts = _router(h2_all, w_router_t_b, router_bias[0])
    dest, emeta = _moe_meta(idx, rank, counts)
    emeta_flat = emeta[:4].reshape(-1)
    n_tb = t_all // TB
    dest_blk = dest.reshape(TOP_K, n_tb, TB).transpose(1, 0, 2).reshape(n_tb, 1, TOP_K * TB)
    xs = _dispatch(emeta_flat, dest_blk, h2_all)
    ys = _experts(emeta_flat, xs, w_exp_gate[0], w_exp_up[0], w_exp_down[0])
    wts_t = wts.T
    y_p = _combine(dest_blk, x1_p, mods_p, wts_t, ys, wsgu_b, wsd_b, g2, b2, 0)
    y_s = _combine(dest_blk, x1_s, mods_s, wts_t, ys, wsgu_b, wsd_b, g2, b2, t_p // TB)

    k_win_s = jnp.concatenate([cache_k[0][:, 1:], k_new.reshape(n_s, 1, N_KV_HEADS, HEAD_DIM)], axis=1)
    v_win_s = jnp.concatenate([cache_v[0][:, 1:], v_new.reshape(n_s, 1, N_KV_HEADS, HEAD_DIM)], axis=1)
    pool_s_out = jnp.concatenate([state_pool[0][:, 1:], u_new[:, None, :]], axis=1)
    return (y_p.reshape(1, t_p, D_MODEL), y_s.reshape(n_s, 1, D_MODEL),
            k_win.reshape(1, 1, WINDOW, N_KV_HEADS, HEAD_DIM), v_win.reshape(1, 1, WINDOW, N_KV_HEADS, HEAD_DIM),
            pool_st.reshape(1, 1, POOL_HALO, D_MODEL),
            k_win_s[None], v_win_s[None], pool_s_out[None])
```

```python
import functools

import jax
import jax.numpy as jnp
from jax import lax
from jax.experimental import pallas as pl
from jax.experimental.pallas import tpu as pltpu

F32, BF16, I32, U32 = jnp.float32, jnp.bfloat16, jnp.int32, jnp.uint32

D_MODEL = 1024
N_HEADS, HEAD_DIM, N_KV_HEADS = 16, 64, 4
KV_WIDTH = N_KV_HEADS * HEAD_DIM
WINDOW = 128
ROPE_THETA, ROT_DIM = 500000.0, 16
PAST_LEN = 16384
POOL_WINDOWS = (2, 4, 8, 16)
POOL_GROUP_DIM = 256
POOL_HALO = 15
N_EXPERTS, TOP_K, N_EXPERT_GROUPS, TOPK_GROUPS = 256, 8, 8, 4
EXPERTS_PER_GROUP = N_EXPERTS // N_EXPERT_GROUPS
EXPERT_DIM, SHARED_DIM = 256, 256
ROUTED_SCALE = 2.5
LN_EPS = 1e-5
DEPTH = 1
ALPHA = (2.0 * DEPTH) ** 0.25
Q0, K0, V0, U0, GA0, GB0, IN_WIDTH = 0, 1024, 1280, 1536, 2560, 3584, 4608

LANES = 128
VMEM_LIMIT_BYTES = 56 * 1024 * 1024

TQ = 256
HALO_ROWS = 16
SB = 16
TB = 128
TBD = 128
ROW_TILE = 8
CH = 256
CH_TILES = CH // ROW_TILE
CH_SHIFT = CH_TILES.bit_length() - 1
EPS = 2
N_XBUF = 4
ZROWS = 256
TAIL_ROWS = -(-(N_EXPERTS * (ROW_TILE - 1) + CH) // ZROWS) * ZROWS
NEG_INF = float("-inf")


def _cparams(n_axes=1):
    return pltpu.CompilerParams(dimension_semantics=("arbitrary",) * n_axes,
                                vmem_limit_bytes=VMEM_LIMIT_BYTES)


def _layer_norm(r, g, b):
    mu = jnp.mean(r, axis=-1, keepdims=True)
    xc = r - mu
    var = jnp.mean(xc * xc, axis=-1, keepdims=True)
    return xc * lax.rsqrt(var + LN_EPS) * g + b


def _silu(x):
    return x * jax.nn.sigmoid(x)


def _pack_bf16_pairs(x):
    c = x.shape[1] // 2
    bits = lax.bitcast_convert_type(x.astype(BF16).astype(F32), U32)
    return (bits[:, :c] >> 16) | (bits[:, c:] & jnp.uint32(0xFFFF0000))


def _unpack_bf16_pairs(w):
    lo = lax.bitcast_convert_type(w << 16, F32)
    hi = lax.bitcast_convert_type(w & jnp.uint32(0xFFFF0000), F32)
    return lo, hi


def _rope128(c, ra, rb, rc):
    return c * ra + pltpu.roll(c, LANES - 8, 1) * rb + pltpu.roll(c, 8, 1) * rc


def _head_variants(x2):
    lo = lax.broadcasted_iota(I32, (1, LANES), 1) < HEAD_DIM
    sw = pltpu.roll(x2, HEAD_DIM, 1)
    zero = jnp.zeros_like(x2)
    return (jnp.where(lo, x2, zero).astype(BF16), jnp.where(lo, zero, sw).astype(BF16),
            jnp.where(lo, sw, zero).astype(BF16), jnp.where(lo, zero, x2).astype(BF16))


def _softmax_sink(s, extra, sink):
    m = jnp.maximum(jnp.max(s, axis=1, keepdims=True), sink)
    if extra is not None:
        m = jnp.maximum(m, extra)
    p = jnp.exp(s - m)
    den = jnp.sum(p, axis=1, keepdims=True) + jnp.exp(sink - m)
    if extra is None:
        return p * (1.0 / den), None
    pe = jnp.exp(extra - m)
    r = 1.0 / (den + pe)
    return p * r, pe * r


def _ada_kernel(c_ref, w_ref, b_ref, o_ref):
    s = _silu(c_ref[...]).astype(BF16)
    o_ref[...] = jnp.dot(s, w_ref[...].astype(BF16), preferred_element_type=F32) + b_ref[...]


def _ada(c_all, w_ada, b_ada):
    rows, n = c_all.shape[0], w_ada.shape[1]
    tn = 1536
    return pl.pallas_call(
        _ada_kernel,
        out_shape=jax.ShapeDtypeStruct((rows, n), F32),
        grid=(n // tn,),
        in_specs=[pl.BlockSpec((rows, D_MODEL), lambda j: (0, 0)),
                  pl.BlockSpec((D_MODEL, tn), lambda j: (0, j)),
                  pl.BlockSpec((1, tn), lambda j: (0, j))],
        out_specs=pl.BlockSpec((rows, tn), lambda j: (0, j)),
        compiler_params=_cparams(),
        name="ada",
    )(c_all, w_ada, b_ada.reshape(1, n))


def _mixer_prompt_kernel(sink_ref, x_ref, mods_ref, ra_ref, rb_ref, rc_ref, win_ref, wpool_ref,
                         pscale_ref, wout_ref, g1_ref, b1_ref,
                         x1_ref, h2_ref, kwin_ref, vwin_ref, pst_ref,
                         q_s, kk_s, vv_s, u_s, attn_s, pool_s):
    i = pl.program_id(0)
    last = pl.num_programs(0) - 1

    @pl.when(i == 0)
    def _():
        kk_s[:, :, 0:WINDOW, :] = jnp.zeros((N_KV_HEADS, 2, WINDOW, LANES), BF16)
        vv_s[:, :, 0:WINDOW, :] = jnp.zeros((N_KV_HEADS, 2, WINDOW, LANES), BF16)
        u_s[0:HALO_ROWS, :] = jnp.zeros((HALO_ROWS, D_MODEL), F32)

    shift1 = mods_ref[:, 0:1024]
    scale1 = mods_ref[:, 1024:2048]
    gate1 = mods_ref[:, 2048:3072]
    shift2 = mods_ref[:, 3072:4096]
    scale2 = mods_ref[:, 4096:5120]

    x = x_ref[...]
    h = (x * (1.0 + scale1) + shift1).astype(BF16)
    ra, rb, rc = ra_ref[...], rb_ref[...], rc_ref[...]

    q = jnp.dot(h, win_ref[:, Q0:K0], preferred_element_type=F32)
    for c in range(D_MODEL // LANES):
        qc = _rope128(q[:, c * LANES:(c + 1) * LANES], ra, rb, rc)
        q_s[:, c * LANES:(c + 1) * LANES] = (qc * (HEAD_DIM ** -0.5)).astype(BF16)

    k = jnp.dot(h, win_ref[:, K0:V0], preferred_element_type=F32)
    v = jnp.dot(h, win_ref[:, V0:U0], preferred_element_type=F32)
    for c in range(KV_WIDTH // LANES):
        kr = _rope128(k[:, c * LANES:(c + 1) * LANES], ra, rb, rc)
        vr = v[:, c * LANES:(c + 1) * LANES]

        @pl.when(i == last)
        def _(kr=kr, vr=vr, c=c):
            kwin_ref[:, c * LANES:(c + 1) * LANES] = kr[TQ - WINDOW:, :]
            vwin_ref[:, c * LANES:(c + 1) * LANES] = vr[TQ - WINDOW:, :]

        ka, kb, kc_, kd = _head_variants(kr)
        va, vb, vc_, vd = _head_variants(vr)
        kk_s[2 * c, 0, WINDOW:, :] = ka
        kk_s[2 * c, 1, WINDOW:, :] = kb
        kk_s[2 * c + 1, 0, WINDOW:, :] = kc_
        kk_s[2 * c + 1, 1, WINDOW:, :] = kd
        vv_s[2 * c, 0, WINDOW:, :] = va
        vv_s[2 * c, 1, WINDOW:, :] = vb
        vv_s[2 * c + 1, 0, WINDOW:, :] = vc_
        vv_s[2 * c + 1, 1, WINDOW:, :] = vd

    u_s[HALO_ROWS:, :] = jnp.dot(h, win_ref[:, U0:GA0], preferred_element_type=F32)

    @pl.when(i == last)
    def _():
        pst_ref[...] = u_s[HALO_ROWS + TQ - POOL_HALO:, :]

    rowpos = i * TQ + lax.broadcasted_iota(I32, (TQ, 1), 0)
    for g, w in enumerate(POOL_WINDOWS):
        cs = slice(g * POOL_GROUP_DIM, (g + 1) * POOL_GROUP_DIM)
        acc = u_s[HALO_ROWS:, cs]
        for m in range(1, w):
            acc = acc + u_s[HALO_ROWS - m:HALO_ROWS - m + TQ, cs]
        cnt = jnp.minimum(w, rowpos + 1).astype(F32)
        d = acc / cnt - u_s[HALO_ROWS:, cs]
        y = jnp.dot(d.astype(BF16), wpool_ref[g], preferred_element_type=F32)
        pool_s[:, cs] = y * pscale_ref[:, cs]

    a_idx = lax.broadcasted_iota(I32, (WINDOW, 4 * WINDOW), 0)
    j_idx = lax.broadcasted_iota(I32, (WINDOW, 4 * WINDOW), 1) & (2 * WINDOW - 1)
    no_prev = jnp.where(i > 0, 0, 4 * WINDOW)
    lo_lanes = lax.broadcasted_iota(I32, (1, LANES), 1) < HEAD_DIM
    key_half = lax.broadcasted_iota(I32, (4 * WINDOW, LANES), 0) >> 8
    lane_half = lax.broadcasted_iota(I32, (4 * WINDOW, LANES), 1) >> 6
    head_ones = (key_half == lane_half).astype(BF16)
    for qb in range(TQ // WINDOW):
        rows = slice(qb * WINDOW, (qb + 1) * WINDOW)
        band = slice(qb * WINDOW, (qb + 2) * WINDOW)
        a_prev = a_idx + no_prev if qb == 0 else a_idx
        mask = jnp.logical_or(jnp.logical_and(j_idx < WINDOW, j_idx >= a_prev),
                              jnp.logical_and(j_idx >= WINDOW, (j_idx - WINDOW) <= a_idx))
        for g in range(N_KV_HEADS):
            k2 = jnp.concatenate([kk_s[g, 0, band, :], kk_s[g, 1, band, :]], axis=0)
            v2 = jnp.concatenate([vv_s[g, 0, band, :], vv_s[g, 1, band, :]], axis=0)
            v2_ones = jnp.concatenate([v2, head_ones], axis=1)
            for p in range(2):
                c = 2 * g + p
                q2 = q_s[rows, c * LANES:(c + 1) * LANES]
                s2 = lax.dot_general(q2, k2, (((1,), (1,)), ((), ())), preferred_element_type=F32)
                s2 = jnp.where(mask, s2, NEG_INF)
                sink_a, sink_b = sink_ref[2 * c], sink_ref[2 * c + 1]
                ma = jnp.maximum(jnp.max(s2[:, :2 * WINDOW], axis=1, keepdims=True), sink_a)
                mb = jnp.maximum(jnp.max(s2[:, 2 * WINDOW:], axis=1, keepdims=True), sink_b)
                p2 = jnp.concatenate([jnp.exp(s2[:, :2 * WINDOW] - ma), jnp.exp(s2[:, 2 * WINDOW:] - mb)],
                                     axis=1).astype(BF16)
                o_den = jnp.dot(p2, v2_ones, preferred_element_type=F32)
                den = o_den[:, LANES:] + jnp.where(lo_lanes, jnp.exp(sink_a - ma), jnp.exp(sink_b - mb))
                attn_s[rows, c * LANES:(c + 1) * LANES] = o_den[:, :LANES] * (1.0 / den)

    kk_s[:, :, 0:WINDOW, :] = kk_s[:, :, TQ:, :]
    vv_s[:, :, 0:WINDOW, :] = vv_s[:, :, TQ:, :]
    u_s[0:HALO_ROWS, :] = u_s[TQ:, :]

    ga = jnp.dot(h, win_ref[:, GA0:GB0], preferred_element_type=F32)
    gb = jnp.dot(h, win_ref[:, GB0:IN_WIDTH], preferred_element_type=F32)
    merged = jax.nn.sigmoid(ga) * attn_s[...] + jax.nn.sigmoid(gb) * pool_s[...]
    y = jnp.dot(merged.astype(BF16), wout_ref[...], preferred_element_type=F32)
    x1 = _layer_norm(ALPHA * x + gate1 * y, g1_ref[...], b1_ref[...])
    x1_ref[...] = x1
    h2_ref[...] = _pack_bf16_pairs(x1 * (1.0 + scale2) + shift2)


def _mixer_prompt(x, mods_p, tabs, w_in_b, sinks, w_pool_b, pool_scale, w_out_b, ln1_g, ln1_b):
    t = x.shape[0]
    full = lambda shape: pl.BlockSpec(shape, lambda i: (0,) * len(shape))
    rowblk = lambda width: pl.BlockSpec((TQ, width), lambda i: (i, 0))
    return pl.pallas_call(
        _mixer_prompt_kernel,
        out_shape=(jax.ShapeDtypeStruct((t, D_MODEL), F32),
                   jax.ShapeDtypeStruct((t, D_MODEL // 2), U32),
                   jax.ShapeDtypeStruct((WINDOW, KV_WIDTH), F32),
                   jax.ShapeDtypeStruct((WINDOW, KV_WIDTH), F32),
                   jax.ShapeDtypeStruct((POOL_HALO, D_MODEL), F32)),
        grid=(t // TQ,),
        in_specs=[pl.BlockSpec(memory_space=pltpu.SMEM),
                  rowblk(D_MODEL), full((1, 6 * D_MODEL)),
                  rowblk(LANES), rowblk(LANES), rowblk(LANES),
                  full((D_MODEL, IN_WIDTH)), full((4, POOL_GROUP_DIM, POOL_GROUP_DIM)),
                  full((1, D_MODEL)), full((D_MODEL, D_MODEL)), full((1, D_MODEL)), full((1, D_MODEL))],
        out_specs=(rowblk(D_MODEL), rowblk(D_MODEL // 2),
                   full((WINDOW, KV_WIDTH)), full((WINDOW, KV_WIDTH)), full((POOL_HALO, D_MODEL))),
        scratch_shapes=[pltpu.VMEM((TQ, D_MODEL), BF16),
                        pltpu.VMEM((N_KV_HEADS, 2, WINDOW + TQ, LANES), BF16),
                        pltpu.VMEM((N_KV_HEADS, 2, WINDOW + TQ, LANES), BF16),
                        pltpu.VMEM((HALO_ROWS + TQ, D_MODEL), F32),
                        pltpu.VMEM((TQ, D_MODEL), F32),
                        pltpu.VMEM((TQ, D_MODEL), F32)],
        compiler_params=_cparams(),
        name="mixer_prompt",
    )(sinks, x, mods_p, *tabs, w_in_b, w_pool_b, pool_scale, w_out_b, ln1_g, ln1_b)


def _sample_proj_kernel(x_ref, mods_ref, ra_ref, rb_ref, rc_ref, win_ref, spt_ref, wpool_ref, pscale_ref,
                        q_ref, k_ref, v_ref, u_ref, pool_ref, ga_ref, gb_ref):
    x = x_ref[...]
    h = (x * (1.0 + mods_ref[:, 1024:2048]) + mods_ref[:, 0:1024]).astype(BF16)
    ra, rb, rc = ra_ref[...], rb_ref[...], rc_ref[...]
    q = jnp.dot(h, win_ref[:, Q0:K0], preferred_element_type=F32)
    for c in range(D_MODEL // LANES):
        qc = _rope128(q[:, c * LANES:(c + 1) * LANES], ra, rb, rc)
        q_ref[:, c * LANES:(c + 1) * LANES] = (qc * (HEAD_DIM ** -0.5)).astype(BF16)
    k = jnp.dot(h, win_ref[:, K0:V0], preferred_element_type=F32)
    for c in range(KV_WIDTH // LANES):
        k_ref[:, c * LANES:(c + 1) * LANES] = _rope128(k[:, c * LANES:(c + 1) * LANES], ra, rb, rc)
    v_ref[...] = jnp.dot(h, win_ref[:, V0:U0], preferred_element_type=F32)
    u = jnp.dot(h, win_ref[:, U0:GA0], preferred_element_type=F32)
    u_ref[...] = u
    for g, w in enumerate(POOL_WINDOWS):
        cs = slice(g * POOL_GROUP_DIM, (g + 1) * POOL_GROUP_DIM)
        acc = u[:, cs]
        for m in range(1, w):
            acc = acc + spt_ref[POOL_HALO - m, :, cs]
        d = acc / float(min(w, PAST_LEN + 1)) - u[:, cs]
        y = jnp.dot(d.astype(BF16), wpool_ref[g], preferred_element_type=F32)
        pool_ref[:, cs] = y * pscale_ref[:, cs]
    ga_ref[...] = jnp.dot(h, win_ref[:, GA0:GB0], preferred_element_type=F32)
    gb_ref[...] = jnp.dot(h, win_ref[:, GB0:IN_WIDTH], preferred_element_type=F32)


def _sample_proj(x_s, mods_s, tabs, w_in_b, sp_t, w_pool_b, pool_scale):
    b = x_s.shape[0]
    sd = lambda w, dt=F32: jax.ShapeDtypeStruct((b, w), dt)
    return pl.pallas_call(
        _sample_proj_kernel,
        out_shape=(sd(D_MODEL, BF16), sd(KV_WIDTH), sd(KV_WIDTH), sd(D_MODEL), sd(D_MODEL),
                   sd(D_MODEL), sd(D_MODEL)),
        compiler_params=pltpu.CompilerParams(vmem_limit_bytes=VMEM_LIMIT_BYTES),
        name="sample_proj",
    )(x_s, mods_s, *tabs, w_in_b, sp_t, w_pool_b, pool_scale)


def _sample_attn_kernel(sink_ref, q_ref, ck_ref, cv_ref, kn_ref, vn_ref, o_ref):
    nkeys = SB * WINDOW
    lo = lax.broadcasted_iota(I32, (1, LANES), 1) < HEAD_DIM
    row_b = lax.broadcasted_iota(I32, (2 * SB, 2 * nkeys), 0) & (SB - 1)
    col_b = (lax.broadcasted_iota(I32, (2 * SB, 2 * nkeys), 1) & (nkeys - 1)) >> 7
    own = row_b == col_b
    first_pair = lax.broadcasted_iota(I32, (2 * SB, 1), 0) < SB
    for c in range(KV_WIDTH // LANES):
        kvar = _head_variants(ck_ref[:, c * LANES:(c + 1) * LANES])
        vvar = _head_variants(cv_ref[:, c * LANES:(c + 1) * LANES])
        kn = kn_ref[:, c * LANES:(c + 1) * LANES].astype(BF16).astype(F32)
        vn = vn_ref[:, c * LANES:(c + 1) * LANES].astype(BF16).astype(F32)
        kn_sw, vn_sw = pltpu.roll(kn, HEAD_DIM, 1), pltpu.roll(vn, HEAD_DIM, 1)
        for e in range(2):
            g = 2 * c + e
            k2 = jnp.concatenate([kvar[2 * e], kvar[2 * e + 1]], axis=0)
            v2 = jnp.concatenate([vvar[2 * e], vvar[2 * e + 1]], axis=0)
            kdup = jnp.where(lo, kn, kn_sw) if e == 0 else jnp.where(lo, kn_sw, kn)
            vdup = jnp.where(lo, vn, vn_sw) if e == 0 else jnp.where(lo, vn_sw, vn)
            qa = q_ref[:, (2 * g) * LANES:(2 * g + 1) * LANES]
            qb = q_ref[:, (2 * g + 1) * LANES:(2 * g + 2) * LANES]
            q4 = jnp.concatenate([qa, qb], axis=0)
            s = lax.dot_general(q4, k2, (((1,), (1,)), ((), ())), preferred_element_type=F32)
            s = jnp.where(own, s, NEG_INF)
            prod = q4.astype(F32) * jnp.concatenate([kdup, kdup], axis=0)
            s_new = (jnp.sum(jnp.where(lo, prod, 0.0), axis=1, keepdims=True),
                     jnp.sum(jnp.where(lo, 0.0, prod), axis=1, keepdims=True))
            ps, pn = [], []
            for hh in range(2):
                sink = jnp.where(first_pair, sink_ref[4 * g + hh], sink_ref[4 * g + 2 + hh])
                p_c, p_n = _softmax_sink(s[:, hh * nkeys:(hh + 1) * nkeys], s_new[hh], sink)
                ps.append(p_c)
                pn.append(p_n)
            p2 = jnp.concatenate(ps, axis=1).astype(BF16)
            o = jnp.dot(p2, v2, preferred_element_type=F32)
            o = o + jnp.where(lo, pn[0], pn[1]) * jnp.concatenate([vdup, vdup], axis=0)
            o_ref[:, (2 * g) * LANES:(2 * g + 1) * LANES] = o[:SB]
            o_ref[:, (2 * g + 1) * LANES:(2 * g + 2) * LANES] = o[SB:]


def _sample_attn(sinks, q, ck2d, cv2d, k_new, v_new):
    b = q.shape[0]
    return pl.pallas_call(
        _sample_attn_kernel,
        out_shape=jax.ShapeDtypeStruct((b, D_MODEL), F32),
        grid=(b // SB,),
        in_specs=[pl.BlockSpec(memory_space=pltpu.SMEM),
                  pl.BlockSpec((SB, D_MODEL), lambda i: (i, 0)),
                  pl.BlockSpec((SB * WINDOW, KV_WIDTH), lambda i: (i, 0)),
                  pl.BlockSpec((SB * WINDOW, KV_WIDTH), lambda i: (i, 0)),
                  pl.BlockSpec((SB, KV_WIDTH), lambda i: (i, 0)),
                  pl.BlockSpec((SB, KV_WIDTH), lambda i: (i, 0))],
        out_specs=pl.BlockSpec((SB, D_MODEL), lambda i: (i, 0)),
        compiler_params=_cparams(),
        name="sample_attn",
    )(sinks, q, ck2d, cv2d, k_new, v_new)


def _sample_merge_kernel(x_ref, mods_ref, attn_ref, pool_ref, ga_ref, gb_ref, wout_ref, g1_ref, b1_ref,
                         x1_ref, h2_ref):
    merged = jax.nn.sigmoid(ga_ref[...]) * attn_ref[...] + jax.nn.sigmoid(gb_ref[...]) * pool_ref[...]
    y = jnp.dot(merged.astype(BF16), wout_ref[...], preferred_element_type=F32)
    x1 = _layer_norm(ALPHA * x_ref[...] + mods_ref[:, 2048:3072] * y, g1_ref[...], b1_ref[...])
    x1_ref[...] = x1
    h2_ref[...] = _pack_bf16_pairs(x1 * (1.0 + mods_ref[:, 4096:5120]) + mods_ref[:, 3072:4096])


def _sample_merge(x_s, mods_s, attn, pool, ga, gb, w_out_b, ln1_g, ln1_b):
    b = x_s.shape[0]
    return pl.pallas_call(
        _sample_merge_kernel,
        out_shape=(jax.ShapeDtypeStruct((b, D_MODEL), F32), jax.ShapeDtypeStruct((b, D_MODEL // 2), U32)),
        compiler_params=pltpu.CompilerParams(vmem_limit_bytes=VMEM_LIMIT_BYTES),
        name="sample_merge",
    )(x_s, mods_s, attn, pool, ga, gb, w_out_b, ln1_g, ln1_b)


def _router_kernel(h2_ref, wrt_ref, bias_ref, idx_ref, wts_ref, rank_ref, cnt_ref, carry_s):
    i = pl.program_id(0)

    @pl.when(i == 0)
    def _():
        carry_s[...] = jnp.zeros_like(carry_s)

    lo, hi = _unpack_bf16_pairs(h2_ref[...])
    nt = (((1,), (1,)), ((), ()))
    half = D_MODEL // 2
    logits = (lax.dot_general(wrt_ref[:, :half], lo.astype(BF16), nt, preferred_element_type=F32)
              + lax.dot_general(wrt_ref[:, half:], hi.astype(BF16), nt, preferred_element_type=F32))
    scores = jax.nn.sigmoid(logits)
    sel = scores + bias_ref[...]

    gi = lax.broadcasted_iota(I32, (EXPERTS_PER_GROUP, TB), 0).astype(F32)
    gscore = []
    for g in range(N_EXPERT_GROUPS):
        blk = sel[g * EXPERTS_PER_GROUP:(g + 1) * EXPERTS_PER_GROUP, :]
        m1 = jnp.max(blk, axis=0, keepdims=True)
        f1 = jnp.min(jnp.where(blk == m1, gi, float(EXPERTS_PER_GROUP)), axis=0, keepdims=True)
        m2 = jnp.max(jnp.where(gi == f1, NEG_INF, blk), axis=0, keepdims=True)
        gscore.append(m1 + m2)

    gsel = [jnp.zeros((1, TB), jnp.bool_) for _ in range(N_EXPERT_GROUPS)]
    for _ in range(TOPK_GROUPS):
        m = gscore[0]
        for g in range(1, N_EXPERT_GROUPS):
            m = jnp.maximum(m, gscore[g])
        found = jnp.zeros((1, TB), jnp.bool_)
        for g in range(N_EXPERT_GROUPS):
            pick = jnp.logical_and(gscore[g] == m, jnp.logical_not(found))
            found = jnp.logical_or(found, pick)
            gsel[g] = jnp.logical_or(gsel[g], pick)
            gscore[g] = jnp.where(pick, NEG_INF, gscore[g])
    cur = jnp.concatenate(
        [jnp.where(gsel[g], sel[g * EXPERTS_PER_GROUP:(g + 1) * EXPERTS_PER_GROUP, :], NEG_INF)
         for g in range(N_EXPERT_GROUPS)], axis=0)

    ei = lax.broadcasted_iota(I32, (N_EXPERTS, TB), 0).astype(F32)
    onehot = jnp.zeros((N_EXPERTS, TB), F32)
    picks, raw_w = [], []
    for _ in range(TOP_K):
        m = jnp.max(cur, axis=0, keepdims=True)
        f = jnp.min(jnp.where(cur == m, ei, float(N_EXPERTS)), axis=0, keepdims=True)
        pick = ei == f
        picks.append(f)
        raw_w.append(jnp.sum(jnp.where(pick, scores, 0.0), axis=0, keepdims=True))
        cur = jnp.where(pick, NEG_INF, cur)
        onehot = onehot + pick.astype(F32)
    wsum = raw_w[0]
    for k in range(1, TOP_K):
        wsum = wsum + raw_w[k]

    tri = (lax.broadcasted_iota(I32, (TB, TB), 0) <= lax.broadcasted_iota(I32, (TB, TB), 1)).astype(BF16)
    incl = jnp.dot(onehot.astype(BF16), tri, preferred_element_type=F32)
    pos = carry_s[...] + incl - onehot
    for k in range(TOP_K):
        idx_ref[k:k + 1, :] = picks[k].astype(I32)
        wts_ref[k:k + 1, :] = raw_w[k] / wsum * ROUTED_SCALE
        rank_ref[k:k + 1, :] = jnp.sum(jnp.where(ei == picks[k], pos, 0.0), axis=0, keepdims=True).astype(I32)
    carry = carry_s[...] + jnp.sum(onehot, axis=1, keepdims=True)
    carry_s[...] = carry
    cnt_ref[...] = carry


def _router(h2_all, w_router_t_b, router_bias):
    t_all = h2_all.shape[0]
    tok = lambda dt: jax.ShapeDtypeStruct((TOP_K, t_all), dt)
    return pl.pallas_call(
        _router_kernel,
        out_shape=(tok(I32), tok(F32), tok(I32), jax.ShapeDtypeStruct((N_EXPERTS, LANES), F32)),
        grid=(t_all // TB,),
        in_specs=[pl.BlockSpec((TB, D_MODEL // 2), lambda i: (i, 0)),
                  pl.BlockSpec((N_EXPERTS, D_MODEL), lambda i: (0, 0)),
                  pl.BlockSpec((N_EXPERTS, 1), lambda i: (0, 0))],
        out_specs=(pl.BlockSpec((TOP_K, TB), lambda i: (0, i)),
                   pl.BlockSpec((TOP_K, TB), lambda i: (0, i)),
                   pl.BlockSpec((TOP_K, TB), lambda i: (0, i)),
                   pl.BlockSpec((N_EXPERTS, LANES), lambda i: (0, 0))),
        scratch_shapes=[pltpu.VMEM((N_EXPERTS, LANES), F32)],
        compiler_params=_cparams(),
        name="router",
    )(h2_all, w_router_t_b, router_bias.reshape(N_EXPERTS, 1))


def _moe_meta_kernel(idx_ref, rank_ref, cnt_ref, dest_ref, emeta_ref, tstart_s):
    i = pl.program_id(0)

    @pl.when(i == 0)
    def _():
        cnt_col = cnt_ref[:, 0:1]
        tiles_col = jnp.floor((cnt_col + float(ROW_TILE - 1)) * (1.0 / ROW_TILE))
        e_r = lax.broadcasted_iota(I32, (N_EXPERTS, N_EXPERTS), 0)
        e_c = lax.broadcasted_iota(I32, (N_EXPERTS, N_EXPERTS), 1)
        diag = e_r == e_c
        tiles_row = jnp.sum(jnp.where(diag, tiles_col, 0.0), axis=0, keepdims=True)
        cnt_row = jnp.sum(jnp.where(diag, cnt_col, 0.0), axis=0, keepdims=True)
        tstart_row = jnp.sum(jnp.where(e_r < e_c, tiles_col, 0.0), axis=0, keepdims=True)
        tstart_col = jnp.sum(jnp.where(e_c < e_r, tiles_row, 0.0), axis=1, keepdims=True)
        tstart_s[...] = jnp.broadcast_to(tstart_col, (N_EXPERTS, LANES))
        emeta_ref[0:1, :] = tstart_row.astype(I32)
        emeta_ref[1:2, :] = tiles_row.astype(I32)
        emeta_ref[2:3, :] = cnt_row.astype(I32)
        later_used = jnp.logical_and(e_r > e_c, tiles_col > 0.0)
        next_used = jnp.min(jnp.where(later_used, e_r.astype(F32), float(N_EXPERTS)), axis=0, keepdims=True)
        emeta_ref[3:4, :] = next_used.astype(I32)
        emeta_ref[4:8, :] = jnp.zeros((4, N_EXPERTS), I32)

    ei = lax.broadcasted_iota(I32, (N_EXPERTS, TB), 0)
    tstart = tstart_s[...]
    for k in range(TOP_K):
        st = jnp.sum(jnp.where(ei == idx_ref[k:k + 1, :], tstart, 0.0), axis=0, keepdims=True)
        dest_ref[k:k + 1, :] = st.astype(I32) * ROW_TILE + rank_ref[k:k + 1, :]


def _moe_meta(idx, rank, counts):
    t_all = idx.shape[1]
    return pl.pallas_call(
        _moe_meta_kernel,
        out_shape=(jax.ShapeDtypeStruct((TOP_K, t_all), I32), jax.ShapeDtypeStruct((8, N_EXPERTS), I32)),
        grid=(t_all // TB,),
        in_specs=[pl.BlockSpec((TOP_K, TB), lambda i: (0, i)),
                  pl.BlockSpec((TOP_K, TB), lambda i: (0, i)),
                  pl.BlockSpec((N_EXPERTS, LANES), lambda i: (0, 0))],
        out_specs=(pl.BlockSpec((TOP_K, TB), lambda i: (0, i)),
                   pl.BlockSpec((8, N_EXPERTS), lambda i: (0, 0))),
        scratch_shapes=[pltpu.VMEM((N_EXPERTS, LANES), F32)],
        compiler_params=_cparams(),
        name="moe_meta",
    )(idx, rank, counts)


def _zero_tail(zbuf, out_ref, zsem, base):
    zbuf[...] = jnp.zeros_like(zbuf)
    copies = [pltpu.make_async_copy(zbuf, out_ref.at[pl.ds(base + j * ZROWS, ZROWS), :], zsem)
              for j in range(TAIL_ROWS // ZROWS)]
    for c in copies:
        c.start()
    for c in copies:
        c.wait()


def _dispatch_kernel(emeta_ref, dest_ref, h2_ref, xs_ref, zbuf, sem, zsem, *, base):
    i = pl.program_id(0)

    @pl.when(i == 0)
    def _():
        _zero_tail(zbuf, xs_ref, zsem, base)

        def gap_rows(e, fn):
            cnt = emeta_ref[2 * N_EXPERTS + e]
            first = emeta_ref[e] * ROW_TILE + cnt
            n_gap = emeta_ref[N_EXPERTS + e] * ROW_TILE - cnt

            def body(r, carry):
                fn(pltpu.make_async_copy(zbuf.at[pl.ds(0, 1), :], xs_ref.at[pl.ds(first + r, 1), :], zsem))
                return carry

            lax.fori_loop(0, n_gap, body, 0)

        def start_gaps(e, carry):
            gap_rows(e, lambda c: c.start())
            return carry

        def wait_gaps(e, carry):
            gap_rows(e, lambda c: c.wait())
            return carry

        lax.fori_loop(0, N_EXPERTS, start_gaps, 0)
        lax.fori_loop(0, N_EXPERTS, wait_gaps, 0)

    def row_copy(a, b, k):
        d = dest_ref[0, 0, k * TBD + a * ROW_TILE + b]
        return pltpu.make_async_copy(h2_ref.at[a, pl.ds(b, 1), :], xs_ref.at[pl.ds(d, 1), :], sem)

    def for_rows(fn):
        def body(a, carry):
            for b in range(ROW_TILE):
                for k in range(TOP_K):
                    fn(row_copy(a, b, k), k)
            return carry

        lax.fori_loop(0, TBD // ROW_TILE, body, 0)

    for_rows(lambda c, k: c.start(priority=k % 2))
    for_rows(lambda c, k: c.wait())


def _dispatch(emeta_flat, dest_blk, h2_all):
    t_all = h2_all.shape[0]
    base = t_all * TOP_K
    grid_spec = pltpu.PrefetchScalarGridSpec(
        num_scalar_prefetch=1,
        grid=(t_all // TBD,),
        in_specs=[pl.BlockSpec((1, 1, TOP_K * TBD), lambda i, m: (i, 0, 0), memory_space=pltpu.SMEM),
                  pl.BlockSpec((TBD // ROW_TILE, ROW_TILE, D_MODEL // 2), lambda i, m: (i, 0, 0))],
        out_specs=pl.BlockSpec(memory_space=pl.ANY),
        scratch_shapes=[pltpu.VMEM((ZROWS, D_MODEL // 2), U32),
                        pltpu.SemaphoreType.DMA(()), pltpu.SemaphoreType.DMA(())],
    )
    return pl.pallas_call(
        functools.partial(_dispatch_kernel, base=base),
        out_shape=jax.ShapeDtypeStruct((base + TAIL_ROWS, D_MODEL // 2), U32),
        grid_spec=grid_spec,
        compiler_params=_cparams(),
        name="dispatch",
    )(emeta_flat, dest_blk, h2_all.reshape(t_all // ROW_TILE, ROW_TILE, D_MODEL // 2))


def _experts_kernel(emeta_ref, xs_ref, wg_ref, wu_ref, wd_ref, ys_ref,
                    xbuf, ybuf, zbuf, wgu_s, wd_s, st, pend_row, pend_n, isem, osem, zsem, *, base):
    step = pl.program_id(0)

    def load(row, slot):
        return pltpu.make_async_copy(xs_ref.at[pl.ds(pl.multiple_of(row, ROW_TILE), CH), :],
                                     xbuf.at[slot], isem.at[slot])

    def request_next(slot):
        ce, cj = st[1], st[2]

        @pl.when(ce < N_EXPERTS)
        def _():
            load(emeta_ref[ce] * ROW_TILE + cj * CH, slot).start()
            c_chunks = (emeta_ref[N_EXPERTS + ce] + (CH_TILES - 1)) >> CH_SHIFT
            more = cj + 1 < c_chunks
            st[1] = jnp.where(more, ce, emeta_ref[3 * N_EXPERTS + ce])
            st[2] = jnp.where(more, cj + 1, 0)

    def out_copies(slot, fn):
        row, n = pend_row[slot], pend_n[slot]

        @pl.when(n == CH_TILES + 1)
        def _():
            fn(pltpu.make_async_copy(ybuf.at[slot], ys_ref.at[pl.ds(pl.multiple_of(row, ROW_TILE), CH), :],
                                     osem.at[slot]))

        def body(t, carry):
            fn(pltpu.make_async_copy(
                ybuf.at[slot, pl.ds(pl.multiple_of(t * ROW_TILE, ROW_TILE), ROW_TILE), :],
                ys_ref.at[pl.ds(pl.multiple_of(row + t * ROW_TILE, ROW_TILE), ROW_TILE), :], osem.at[slot]))
            return carry

        lax.fori_loop(0, jnp.where(n <= CH_TILES, n, 0), body, 0)

    def run_expert(e, sub):
        _expert_body(e, sub, emeta_ref, wg_ref, wu_ref, wd_ref, xbuf, ybuf, wgu_s, wd_s, st, pend_row, pend_n,
                     load, request_next, out_copies)

    @pl.when(step == 0)
    def _():
        _zero_tail(zbuf, ys_ref, zsem, base)
        st[0] = 0
        st[1] = jnp.where(emeta_ref[N_EXPERTS] > 0, 0, emeta_ref[3 * N_EXPERTS])
        st[2] = 0
        pend_n[0] = 0
        pend_n[1] = 0
        for s in range(N_XBUF - 1):
            request_next(s)

    for sub in range(EPS):
        run_expert(step * EPS + sub, sub)

    @pl.when(step == pl.num_programs(0) - 1)
    def _():
        for s in range(2):
            out_copies(s, lambda c: c.wait())
            pend_n[s] = 0


def _expert_body(e, sub, emeta_ref, wg_ref, wu_ref, wd_ref, xbuf, ybuf, wgu_s, wd_s, st, pend_row, pend_n,
                 load, request_next, out_copies):
    row0 = emeta_ref[e] * ROW_TILE
    n_tiles = emeta_ref[N_EXPERTS + e]
    n_chunks = (n_tiles + (CH_TILES - 1)) >> CH_SHIFT

    @pl.when(n_tiles > 0)
    def _():
        wgu_s[:, :EXPERT_DIM] = wg_ref[sub].astype(BF16)
        wgu_s[:, EXPERT_DIM:] = wu_ref[sub].astype(BF16)
        wd_s[...] = wd_ref[sub].astype(BF16)
        g0 = st[0]

        def chunk(j, carry):
            g = g0 + j
            slot = g & (N_XBUF - 1)
            load(row0 + j * CH, slot).wait()
            request_next((g + N_XBUF - 1) & (N_XBUF - 1))

            lo, hi = _unpack_bf16_pairs(xbuf[slot])
            half = D_MODEL // 2
            gu = (jnp.dot(lo.astype(BF16), wgu_s[:half, :], preferred_element_type=F32)
                  + jnp.dot(hi.astype(BF16), wgu_s[half:, :], preferred_element_type=F32))
            act = _silu(gu[:, :EXPERT_DIM]) * gu[:, EXPERT_DIM:]
            y = _pack_bf16_pairs(jnp.dot(act.astype(BF16), wd_s[...], preferred_element_type=F32))

            oslot = g & 1
            out_copies(oslot, lambda c: c.wait())
            ybuf[oslot] = y
            pend_row[oslot] = row0 + j * CH
            pend_n[oslot] = jnp.where((j + 1) * CH_TILES <= n_tiles, CH_TILES + 1, n_tiles - j * CH_TILES)
            out_copies(oslot, lambda c: c.start())
            return carry

        lax.fori_loop(0, n_chunks, chunk, 0)
        st[0] = g0 + n_chunks


def _experts(emeta_flat, xs, w_gate, w_up, w_down):
    base = xs.shape[0] - TAIL_ROWS
    w_map = lambda e, m: (e, 0, 0)
    grid_spec = pltpu.PrefetchScalarGridSpec(
        num_scalar_prefetch=1,
        grid=(N_EXPERTS // EPS,),
        in_specs=[pl.BlockSpec(memory_space=pl.ANY),
                  pl.BlockSpec((EPS, D_MODEL, EXPERT_DIM), w_map),
                  pl.BlockSpec((EPS, D_MODEL, EXPERT_DIM), w_map),
                  pl.BlockSpec((EPS, EXPERT_DIM, D_MODEL), w_map)],
        out_specs=pl.BlockSpec(memory_space=pl.ANY),
        scratch_shapes=[pltpu.VMEM((N_XBUF, CH, D_MODEL // 2), U32),
                        pltpu.VMEM((2, CH, D_MODEL // 2), U32),
                        pltpu.VMEM((ZROWS, D_MODEL // 2), U32),
                        pltpu.VMEM((D_MODEL, 2 * EXPERT_DIM), BF16),
                        pltpu.VMEM((EXPERT_DIM, D_MODEL), BF16),
                        pltpu.SMEM((3,), I32), pltpu.SMEM((2,), I32), pltpu.SMEM((2,), I32),
                        pltpu.SemaphoreType.DMA((N_XBUF,)), pltpu.SemaphoreType.DMA((2,)),
                        pltpu.SemaphoreType.DMA(())],
    )
    return pl.pallas_call(
        functools.partial(_experts_kernel, base=base),
        out_shape=jax.ShapeDtypeStruct(xs.shape, U32),
        grid_spec=grid_spec,
        compiler_params=_cparams(),
        name="experts",
    )(emeta_flat, xs, w_gate, w_up, w_down)


def _combine_kernel(dest_ref, dnext_ref, x1_ref, mods_ref, wts_ref, ys_ref, wsgu_ref, wsd_ref, g2_ref, b2_ref,
                    out_ref, buf, sem):
    i = pl.program_id(0)
    slot = i & 1

    def row_copy(dref, s, a, b, k):
        return pltpu.make_async_copy(ys_ref.at[pl.ds(dref[0, 0, k * TB + a * ROW_TILE + b], 1), :],
                                     buf.at[s, k, a, pl.ds(b, 1), :], sem.at[s])

    def for_rows(fn):
        def body(a, carry):
            for b in range(ROW_TILE):
                for k in range(TOP_K):
                    fn(a, b, k)
            return carry

        lax.fori_loop(0, TB // ROW_TILE, body, 0)

    def gather(dref, s):
        for a in range(TB // ROW_TILE):
            for b in range(ROW_TILE):
                for k in range(TOP_K):
                    row_copy(dref, s, a, b, k).start(priority=k % 2)

    @pl.when(i == 0)
    def _():
        gather(dest_ref, 0)

    @pl.when(i + 1 < pl.num_programs(0))
    def _():
        for s in range(2):
            @pl.when(slot == 1 - s)
            def _(s=s):
                gather(dnext_ref, s)

    x1 = x1_ref[...]
    h2 = (x1 * (1.0 + mods_ref[:, 4096:5120]) + mods_ref[:, 3072:4096]).astype(BF16)
    gu = jnp.dot(h2, wsgu_ref[...], preferred_element_type=F32)
    act = _silu(gu[:, :SHARED_DIM]) * gu[:, SHARED_DIM:]
    shared = jnp.dot(act.astype(BF16), wsd_ref[...], preferred_element_type=F32)

    for_rows(lambda a, b, k: row_copy(dest_ref, slot, a, b, k).wait())

    acc_lo = jnp.zeros((TB, D_MODEL // 2), F32)
    acc_hi = jnp.zeros((TB, D_MODEL // 2), F32)
    for k in range(TOP_K):
        lo, hi = _unpack_bf16_pairs(buf[slot, k].reshape(TB, D_MODEL // 2))
        wk = wts_ref[:, k:k + 1]
        acc_lo = acc_lo + wk * lo
        acc_hi = acc_hi + wk * hi
    m = shared + jnp.concatenate([acc_lo, acc_hi], axis=1)
    out_ref[...] = _layer_norm(ALPHA * x1 + mods_ref[:, 5120:6144] * m, g2_ref[...], b2_ref[...])


def _combine(dest_blk, x1, mods, wts_t, ys, wsgu_b, wsd_b, ln2_g, ln2_b, blk_off):
    t = x1.shape[0]
    per_row = mods.shape[0] != 1
    mods_spec = (pl.BlockSpec((TB, 6 * D_MODEL), lambda i: (i, 0)) if per_row
                 else pl.BlockSpec((1, 6 * D_MODEL), lambda i: (0, 0)))
    full = lambda shape: pl.BlockSpec(shape, lambda i: (0,) * len(shape))
    n_steps = t // TB
    return pl.pallas_call(
        _combine_kernel,
        out_shape=jax.ShapeDtypeStruct((t, D_MODEL), F32),
        grid=(n_steps,),
        in_specs=[pl.BlockSpec((1, 1, TOP_K * TB), lambda i: (i + blk_off, 0, 0), memory_space=pltpu.SMEM),
                  pl.BlockSpec((1, 1, TOP_K * TB), lambda i: (jnp.minimum(i + 1, n_steps - 1) + blk_off, 0, 0),
                               memory_space=pltpu.SMEM),
                  pl.BlockSpec((TB, D_MODEL), lambda i: (i, 0)),
                  mods_spec,
                  pl.BlockSpec((TB, TOP_K), lambda i: (i + blk_off, 0)),
                  pl.BlockSpec(memory_space=pl.ANY),
                  full((D_MODEL, 2 * SHARED_DIM)), full((SHARED_DIM, D_MODEL)),
                  full((1, D_MODEL)), full((1, D_MODEL))],
        out_specs=pl.BlockSpec((TB, D_MODEL), lambda i: (i, 0)),
        scratch_shapes=[pltpu.VMEM((2, TOP_K, TB // ROW_TILE, ROW_TILE, D_MODEL // 2), U32),
                        pltpu.SemaphoreType.DMA((2,))],
        compiler_params=_cparams(),
        name="combine",
    )(dest_blk, dest_blk, x1, mods, wts_t, ys, wsgu_b, wsd_b, ln2_g, ln2_b)


def _rope_tables(pos):
    half = ROT_DIM // 2
    inv = ROPE_THETA ** (-jnp.arange(0, ROT_DIM, 2, dtype=F32) / ROT_DIM)
    ang = pos.astype(F32)[:, None] * inv[None, :]
    cos, sin = lax.optimization_barrier((jnp.cos(ang), jnp.sin(ang)))
    n = pos.shape[0]
    ones = jnp.ones((n, HEAD_DIM - ROT_DIM), F32)
    z = lambda w: jnp.zeros((n, w), F32)
    ra = jnp.concatenate([cos, cos, ones], axis=1)
    rb = jnp.concatenate([-sin, z(HEAD_DIM - half)], axis=1)
    rc = jnp.concatenate([z(half), sin, z(HEAD_DIM - ROT_DIM)], axis=1)
    return tuple(jnp.tile(t, (1, LANES // HEAD_DIM)) for t in (ra, rb, rc))


def kernel(x_prompt, x_sample, cache_k, cache_v, state_pool, c_prompt, c_sample, w_ada, b_ada, w_in, attn_sinks, w_pool, pool_scale, w_out, ln1_g, ln1_b, w_router, router_bias, w_exp_gate, w_exp_up, w_exp_down, w_sh_gate, w_sh_up, w_sh_down, ln2_g, ln2_b):
    assert w_ada.shape[0] == DEPTH == 1 and x_prompt.shape[0] == 1 and x_sample.shape[1] == 1
    t_p, n_s = x_prompt.shape[1], x_sample.shape[0]
    assert t_p % TQ == 0 and n_s % SB == 0 and n_s == TB and cache_k.shape[2] == WINDOW
    t_all = t_p + n_s
    assert t_all % TBD == 0

    row = lambda a: a[0].reshape(1, -1)
    w_in_b, w_out_b, w_pool_b = w_in[0].astype(BF16), w_out[0].astype(BF16), w_pool[0].astype(BF16)
    wsgu_b = jnp.concatenate([w_sh_gate[0], w_sh_up[0]], axis=1).astype(BF16)
    wsd_b = w_sh_down[0].astype(BF16)
    w_router_t_b = w_router[0].T.astype(BF16)
    sinks = attn_sinks[0]
    pscale, g1, b1, g2, b2 = row(pool_scale), row(ln1_g), row(ln1_b), row(ln2_g), row(ln2_b)

    x_s = x_sample.reshape(n_s, D_MODEL)
    c_all = jnp.concatenate([c_sample, c_prompt, jnp.zeros((7, D_MODEL), F32)], axis=0)
    mods = _ada(c_all, w_ada[0], b_ada[0])
    mods_s, mods_p = mods[:n_s], mods[n_s:n_s + 1]

    x_p = x_prompt.reshape(t_p, D_MODEL)
    x1_p, h2_p, k_win, v_win, pool_st = _mixer_prompt(
        x_p, mods_p, _rope_tables(jnp.arange(t_p)), w_in_b, sinks, w_pool_b, pscale, w_out_b, g1, b1)

    sp_t = jnp.transpose(state_pool[0], (1, 0, 2))
    q_s, k_new, v_new, u_new, pool_s, ga_s, gb_s = _sample_proj(
        x_s, mods_s, _rope_tables(jnp.full((1,), PAST_LEN)), w_in_b, sp_t, w_pool_b, pscale)
    attn_s = _sample_attn(sinks, q_s, cache_k[0].reshape(n_s * WINDOW, KV_WIDTH),
                          cache_v[0].reshape(n_s * WINDOW, KV_WIDTH), k_new, v_new)
    x1_s, h2_s = _sample_merge(x_s, mods_s, attn_s, pool_s, ga_s, gb_s, w_out_b, g1, b1)

    h2_all = jnp.concatenate([h2_p, h2_s], axis=0)
    idx, wts, rank, counts = _router(h2_all, w_router_t_b, router_bias[0])
    dest, emeta = _moe_meta(idx, rank, counts)
    emeta_flat = emeta[:4].reshape(-1)
    n_tb = t_all // TB
    dest_blk = dest.reshape(TOP_K, n_tb, TB).transpose(1, 0, 2).reshape(n_tb, 1, TOP_K * TB)
    n_tbd = t_all // TBD
    dest_blk_d = dest.reshape(TOP_K, n_tbd, TBD).transpose(1, 0, 2).reshape(n_tbd, 1, TOP_K * TBD)
    xs = _dispatch(emeta_flat, dest_blk_d, h2_all)
    ys = _experts(emeta_flat, xs, w_exp_gate[0], w_exp_up[0], w_exp_down[0])
    wts_t = wts.T
    y_p = _combine(dest_blk, x1_p, mods_p, wts_t, ys, wsgu_b, wsd_b, g2, b2, 0)
    y_s = _combine(dest_blk, x1_s, mods_s, wts_t, ys, wsgu_b, wsd_b, g2, b2, t_p // TB)

    k_win_s = jnp.concatenate([cache_k[0][:, 1:], k_new.reshape(n_s, 1, N_KV_HEADS, HEAD_DIM)], axis=1)
    v_win_s = jnp.concatenate([cache_v[0][:, 1:], v_new.reshape(n_s, 1, N_KV_HEADS, HEAD_DIM)], axis=1)
    pool_s_out = jnp.concatenate([state_pool[0][:, 1:], u_new[:, None, :]], axis=1)
    return (y_p.reshape(1, t_p, D_MODEL), y_s.reshape(n_s, 1, D_MODEL),
            k_win.reshape(1, 1, WINDOW, N_KV_HEADS, HEAD_DIM), v_win.reshape(1, 1, WINDOW, N_KV_HEADS, HEAD_DIM),
            pool_st.reshape(1, 1, POOL_HALO, D_MODEL),
            k_win_s[None], v_win_s[None], pool_s_out[None])
```

```python
import functools

import jax
import jax.numpy as jnp
from jax import lax
from jax.experimental import pallas as pl
from jax.experimental.pallas import tpu as pltpu

F32, BF16, I32, U32 = jnp.float32, jnp.bfloat16, jnp.int32, jnp.uint32

D_MODEL = 1024
N_HEADS, HEAD_DIM, N_KV_HEADS = 16, 64, 4
KV_WIDTH = N_KV_HEADS * HEAD_DIM
WINDOW = 128
ROPE_THETA, ROT_DIM = 500000.0, 16
PAST_LEN = 16384
POOL_WINDOWS = (2, 4, 8, 16)
POOL_GROUP_DIM = 256
POOL_HALO = 15
N_EXPERTS, TOP_K, N_EXPERT_GROUPS, TOPK_GROUPS = 256, 8, 8, 4
EXPERTS_PER_GROUP = N_EXPERTS // N_EXPERT_GROUPS
EXPERT_DIM, SHARED_DIM = 256, 256
ROUTED_SCALE = 2.5
LN_EPS = 1e-5
DEPTH = 1
ALPHA = (2.0 * DEPTH) ** 0.25
Q0, K0, V0, U0, GA0, GB0, IN_WIDTH = 0, 1024, 1280, 1536, 2560, 3584, 4608

LANES = 128
VMEM_LIMIT_BYTES = 56 * 1024 * 1024

TQ = 256
HALO_ROWS = 16
SB = 16
TB = 128
TBD = 128
ROW_TILE = 8
CH = 256
CH_TILES = CH // ROW_TILE
CH_SHIFT = CH_TILES.bit_length() - 1
EPS = 2
N_XBUF = 4
ZROWS = 256
TAIL_ROWS = -(-(N_EXPERTS * (ROW_TILE - 1) + CH) // ZROWS) * ZROWS
NEG_INF = float("-inf")


def _cparams(n_axes=1):
    return pltpu.CompilerParams(dimension_semantics=("arbitrary",) * n_axes,
                                vmem_limit_bytes=VMEM_LIMIT_BYTES)


def _layer_norm(r, g, b):
    mu = jnp.mean(r, axis=-1, keepdims=True)
    xc = r - mu
    var = jnp.mean(xc * xc, axis=-1, keepdims=True)
    return xc * lax.rsqrt(var + LN_EPS) * g + b


def _silu(x):
    return x * jax.nn.sigmoid(x)


def _pack_bf16_pairs(x):
    c = x.shape[1] // 2
    bits = lax.bitcast_convert_type(x.astype(BF16).astype(F32), U32)
    return (bits[:, :c] >> 16) | (bits[:, c:] & jnp.uint32(0xFFFF0000))


def _unpack_bf16_pairs(w):
    lo = lax.bitcast_convert_type(w << 16, F32)
    hi = lax.bitcast_convert_type(w & jnp.uint32(0xFFFF0000), F32)
    return lo, hi


def _rope128(c, ra, rb, rc):
    return c * ra + pltpu.roll(c, LANES - 8, 1) * rb + pltpu.roll(c, 8, 1) * rc


def _head_variants(x2):
    lo = lax.broadcasted_iota(I32, (1, LANES), 1) < HEAD_DIM
    sw = pltpu.roll(x2, HEAD_DIM, 1)
    zero = jnp.zeros_like(x2)
    return (jnp.where(lo, x2, zero).astype(BF16), jnp.where(lo, zero, sw).astype(BF16),
            jnp.where(lo, sw, zero).astype(BF16), jnp.where(lo, zero, x2).astype(BF16))


def _softmax_sink(s, extra, sink):
    m = jnp.maximum(jnp.max(s, axis=1, keepdims=True), sink)
    if extra is not None:
        m = jnp.maximum(m, extra)
    p = jnp.exp(s - m)
    den = jnp.sum(p, axis=1, keepdims=True) + jnp.exp(sink - m)
    if extra is None:
        return p * (1.0 / den), None
    pe = jnp.exp(extra - m)
    r = 1.0 / (den + pe)
    return p * r, pe * r


def _ada_kernel(c_ref, w_ref, b_ref, o_ref):
    s = _silu(c_ref[...]).astype(BF16)
    o_ref[...] = jnp.dot(s, w_ref[...].astype(BF16), preferred_element_type=F32) + b_ref[...]


def _ada(c_all, w_ada, b_ada):
    rows, n = c_all.shape[0], w_ada.shape[1]
    tn = 1536
    return pl.pallas_call(
        _ada_kernel,
        out_shape=jax.ShapeDtypeStruct((rows, n), F32),
        grid=(n // tn,),
        in_specs=[pl.BlockSpec((rows, D_MODEL), lambda j: (0, 0)),
                  pl.BlockSpec((D_MODEL, tn), lambda j: (0, j)),
                  pl.BlockSpec((1, tn), lambda j: (0, j))],
        out_specs=pl.BlockSpec((rows, tn), lambda j: (0, j)),
        compiler_params=_cparams(),
        name="ada",
    )(c_all, w_ada, b_ada.reshape(1, n))


def _mixer_prompt_kernel(sink_ref, x_ref, mods_ref, ra_ref, rb_ref, rc_ref, win_ref, wpool_ref,
                         pscale_ref, wout_ref, g1_ref, b1_ref,
                         x1_ref, h2_ref, kwin_ref, vwin_ref, pst_ref,
                         q_s, kk_s, vv_s, u_s, attn_s, pool_s):
    i = pl.program_id(0)
    last = pl.num_programs(0) - 1

    @pl.when(i == 0)
    def _():
        kk_s[:, :, 0:WINDOW, :] = jnp.zeros((N_KV_HEADS, 2, WINDOW, LANES), BF16)
        vv_s[:, :, 0:WINDOW, :] = jnp.zeros((N_KV_HEADS, 2, WINDOW, LANES), BF16)
        u_s[0:HALO_ROWS, :] = jnp.zeros((HALO_ROWS, D_MODEL), F32)

    shift1 = mods_ref[:, 0:1024]
    scale1 = mods_ref[:, 1024:2048]
    gate1 = mods_ref[:, 2048:3072]
    shift2 = mods_ref[:, 3072:4096]
    scale2 = mods_ref[:, 4096:5120]

    x = x_ref[...]
    h = (x * (1.0 + scale1) + shift1).astype(BF16)
    ra, rb, rc = ra_ref[...], rb_ref[...], rc_ref[...]

    q = jnp.dot(h, win_ref[:, Q0:K0], preferred_element_type=F32)
    for c in range(D_MODEL // LANES):
        qc = _rope128(q[:, c * LANES:(c + 1) * LANES], ra, rb, rc)
        q_s[:, c * LANES:(c + 1) * LANES] = (qc * (HEAD_DIM ** -0.5)).astype(BF16)

    k = jnp.dot(h, win_ref[:, K0:V0], preferred_element_type=F32)
    v = jnp.dot(h, win_ref[:, V0:U0], preferred_element_type=F32)
    for c in range(KV_WIDTH // LANES):
        kr = _rope128(k[:, c * LANES:(c + 1) * LANES], ra, rb, rc)
        vr = v[:, c * LANES:(c + 1) * LANES]

        @pl.when(i == last)
        def _(kr=kr, vr=vr, c=c):
            kwin_ref[:, c * LANES:(c + 1) * LANES] = kr[TQ - WINDOW:, :]
            vwin_ref[:, c * LANES:(c + 1) * LANES] = vr[TQ - WINDOW:, :]

        ka, kb, kc_, kd = _head_variants(kr)
        va, vb, vc_, vd = _head_variants(vr)
        kk_s[2 * c, 0, WINDOW:, :] = ka
        kk_s[2 * c, 1, WINDOW:, :] = kb
        kk_s[2 * c + 1, 0, WINDOW:, :] = kc_
        kk_s[2 * c + 1, 1, WINDOW:, :] = kd
        vv_s[2 * c, 0, WINDOW:, :] = va
        vv_s[2 * c, 1, WINDOW:, :] = vb
        vv_s[2 * c + 1, 0, WINDOW:, :] = vc_
        vv_s[2 * c + 1, 1, WINDOW:, :] = vd

    u_s[HALO_ROWS:, :] = jnp.dot(h, win_ref[:, U0:GA0], preferred_element_type=F32)

    @pl.when(i == last)
    def _():
        pst_ref[...] = u_s[HALO_ROWS + TQ - POOL_HALO:, :]

    rowpos = i * TQ + lax.broadcasted_iota(I32, (TQ, 1), 0)
    for g, w in enumerate(POOL_WINDOWS):
        cs = slice(g * POOL_GROUP_DIM, (g + 1) * POOL_GROUP_DIM)
        acc = u_s[HALO_ROWS:, cs]
        for m in range(1, w):
            acc = acc + u_s[HALO_ROWS - m:HALO_ROWS - m + TQ, cs]
        cnt = jnp.minimum(w, rowpos + 1).astype(F32)
        d = acc / cnt - u_s[HALO_ROWS:, cs]
        y = jnp.dot(d.astype(BF16), wpool_ref[g], preferred_element_type=F32)
        pool_s[:, cs] = y * pscale_ref[:, cs]

    a_idx = lax.broadcasted_iota(I32, (WINDOW, 4 * WINDOW), 0)
    j_idx = lax.broadcasted_iota(I32, (WINDOW, 4 * WINDOW), 1) & (2 * WINDOW - 1)
    no_prev = jnp.where(i > 0, 0, 4 * WINDOW)
    lo_lanes = lax.broadcasted_iota(I32, (1, LANES), 1) < HEAD_DIM
    key_half = lax.broadcasted_iota(I32, (4 * WINDOW, LANES), 0) >> 8
    lane_half = lax.broadcasted_iota(I32, (4 * WINDOW, LANES), 1) >> 6
    head_ones = (key_half == lane_half).astype(BF16)
    for qb in range(TQ // WINDOW):
        rows = slice(qb * WINDOW, (qb + 1) * WINDOW)
        band = slice(qb * WINDOW, (qb + 2) * WINDOW)
        a_prev = a_idx + no_prev if qb == 0 else a_idx
        mask = jnp.logical_or(jnp.logical_and(j_idx < WINDOW, j_idx >= a_prev),
                              jnp.logical_and(j_idx >= WINDOW, (j_idx - WINDOW) <= a_idx))
        for g in range(N_KV_HEADS):
            k2 = jnp.concatenate([kk_s[g, 0, band, :], kk_s[g, 1, band, :]], axis=0)
            v2 = jnp.concatenate([vv_s[g, 0, band, :], vv_s[g, 1, band, :]], axis=0)
            v2_ones = jnp.concatenate([v2, head_ones], axis=1)
            for p in range(2):
                c = 2 * g + p
                q2 = q_s[rows, c * LANES:(c + 1) * LANES]
                s2 = lax.dot_general(q2, k2, (((1,), (1,)), ((), ())), preferred_element_type=F32)
                s2 = jnp.where(mask, s2, NEG_INF)
                sink_a, sink_b = sink_ref[2 * c], sink_ref[2 * c + 1]
                ma = jnp.maximum(jnp.max(s2[:, :2 * WINDOW], axis=1, keepdims=True), sink_a)
                mb = jnp.maximum(jnp.max(s2[:, 2 * WINDOW:], axis=1, keepdims=True), sink_b)
                p2 = jnp.concatenate([jnp.exp(s2[:, :2 * WINDOW] - ma), jnp.exp(s2[:, 2 * WINDOW:] - mb)],
                                     axis=1).astype(BF16)
                o_den = jnp.dot(p2, v2_ones, preferred_element_type=F32)
                den = o_den[:, LANES:] + jnp.where(lo_lanes, jnp.exp(sink_a - ma), jnp.exp(sink_b - mb))
                attn_s[rows, c * LANES:(c + 1) * LANES] = o_den[:, :LANES] * (1.0 / den)

    kk_s[:, :, 0:WINDOW, :] = kk_s[:, :, TQ:, :]
    vv_s[:, :, 0:WINDOW, :] = vv_s[:, :, TQ:, :]
    u_s[0:HALO_ROWS, :] = u_s[TQ:, :]

    ga = jnp.dot(h, win_ref[:, GA0:GB0], preferred_element_type=F32)
    gb = jnp.dot(h, win_ref[:, GB0:IN_WIDTH], preferred_element_type=F32)
    merged = jax.nn.sigmoid(ga) * attn_s[...] + jax.nn.sigmoid(gb) * pool_s[...]
    y = jnp.dot(merged.astype(BF16), wout_ref[...], preferred_element_type=F32)
    x1 = _layer_norm(ALPHA * x + gate1 * y, g1_ref[...], b1_ref[...])
    x1_ref[...] = x1
    h2_ref[...] = _pack_bf16_pairs(x1 * (1.0 + scale2) + shift2)


def _mixer_prompt(x, mods_p, tabs, w_in_b, sinks, w_pool_b, pool_scale, w_out_b, ln1_g, ln1_b):
    t = x.shape[0]
    full = lambda shape: pl.BlockSpec(shape, lambda i: (0,) * len(shape))
    rowblk = lambda width: pl.BlockSpec((TQ, width), lambda i: (i, 0))
    return pl.pallas_call(
        _mixer_prompt_kernel,
        out_shape=(jax.ShapeDtypeStruct((t, D_MODEL), F32),
                   jax.ShapeDtypeStruct((t, D_MODEL // 2), U32),
                   jax.ShapeDtypeStruct((WINDOW, KV_WIDTH), F32),
                   jax.ShapeDtypeStruct((WINDOW, KV_WIDTH), F32),
                   jax.ShapeDtypeStruct((POOL_HALO, D_MODEL), F32)),
        grid=(t // TQ,),
        in_specs=[pl.BlockSpec(memory_space=pltpu.SMEM),
                  rowblk(D_MODEL), full((1, 6 * D_MODEL)),
                  rowblk(LANES), rowblk(LANES), rowblk(LANES),
                  full((D_MODEL, IN_WIDTH)), full((4, POOL_GROUP_DIM, POOL_GROUP_DIM)),
                  full((1, D_MODEL)), full((D_MODEL, D_MODEL)), full((1, D_MODEL)), full((1, D_MODEL))],
        out_specs=(rowblk(D_MODEL), rowblk(D_MODEL // 2),
                   full((WINDOW, KV_WIDTH)), full((WINDOW, KV_WIDTH)), full((POOL_HALO, D_MODEL))),
        scratch_shapes=[pltpu.VMEM((TQ, D_MODEL), BF16),
                        pltpu.VMEM((N_KV_HEADS, 2, WINDOW + TQ, LANES), BF16),
                        pltpu.VMEM((N_KV_HEADS, 2, WINDOW + TQ, LANES), BF16),
                        pltpu.VMEM((HALO_ROWS + TQ, D_MODEL), F32),
                        pltpu.VMEM((TQ, D_MODEL), F32),
                        pltpu.VMEM((TQ, D_MODEL), F32)],
        compiler_params=_cparams(),
        name="mixer_prompt",
    )(sinks, x, mods_p, *tabs, w_in_b, w_pool_b, pool_scale, w_out_b, ln1_g, ln1_b)


def _sample_proj_kernel(x_ref, mods_ref, ra_ref, rb_ref, rc_ref, win_ref, spt_ref, wpool_ref, pscale_ref,
                        q_ref, k_ref, v_ref, u_ref, pool_ref, ga_ref, gb_ref):
    x = x_ref[...]
    h = (x * (1.0 + mods_ref[:, 1024:2048]) + mods_ref[:, 0:1024]).astype(BF16)
    ra, rb, rc = ra_ref[...], rb_ref[...], rc_ref[...]
    q = jnp.dot(h, win_ref[:, Q0:K0], preferred_element_type=F32)
    for c in range(D_MODEL // LANES):
        qc = _rope128(q[:, c * LANES:(c + 1) * LANES], ra, rb, rc)
        q_ref[:, c * LANES:(c + 1) * LANES] = (qc * (HEAD_DIM ** -0.5)).astype(BF16)
    k = jnp.dot(h, win_ref[:, K0:V0], preferred_element_type=F32)
    for c in range(KV_WIDTH // LANES):
        k_ref[:, c * LANES:(c + 1) * LANES] = _rope128(k[:, c * LANES:(c + 1) * LANES], ra, rb, rc)
    v_ref[...] = jnp.dot(h, win_ref[:, V0:U0], preferred_element_type=F32)
    u = jnp.dot(h, win_ref[:, U0:GA0], preferred_element_type=F32)
    u_ref[...] = u
    for g, w in enumerate(POOL_WINDOWS):
        cs = slice(g * POOL_GROUP_DIM, (g + 1) * POOL_GROUP_DIM)
        acc = u[:, cs]
        for m in range(1, w):
            acc = acc + spt_ref[POOL_HALO - m, :, cs]
        d = acc / float(min(w, PAST_LEN + 1)) - u[:, cs]
        y = jnp.dot(d.astype(BF16), wpool_ref[g], preferred_element_type=F32)
        pool_ref[:, cs] = y * pscale_ref[:, cs]
    ga_ref[...] = jnp.dot(h, win_ref[:, GA0:GB0], preferred_element_type=F32)
    gb_ref[...] = jnp.dot(h, win_ref[:, GB0:IN_WIDTH], preferred_element_type=F32)


def _sample_proj(x_s, mods_s, tabs, w_in_b, sp_t, w_pool_b, pool_scale):
    b = x_s.shape[0]
    sd = lambda w, dt=F32: jax.ShapeDtypeStruct((b, w), dt)
    return pl.pallas_call(
        _sample_proj_kernel,
        out_shape=(sd(D_MODEL, BF16), sd(KV_WIDTH), sd(KV_WIDTH), sd(D_MODEL), sd(D_MODEL),
                   sd(D_MODEL), sd(D_MODEL)),
        compiler_params=pltpu.CompilerParams(vmem_limit_bytes=VMEM_LIMIT_BYTES),
        name="sample_proj",
    )(x_s, mods_s, *tabs, w_in_b, sp_t, w_pool_b, pool_scale)


def _sample_attn_kernel(sink_ref, q_ref, ck_ref, cv_ref, kn_ref, vn_ref, o_ref):
    nkeys = SB * WINDOW
    lo = lax.broadcasted_iota(I32, (1, LANES), 1) < HEAD_DIM
    row_b = lax.broadcasted_iota(I32, (2 * SB, 2 * nkeys), 0) & (SB - 1)
    col_b = (lax.broadcasted_iota(I32, (2 * SB, 2 * nkeys), 1) & (nkeys - 1)) >> 7
    own = row_b == col_b
    first_pair = lax.broadcasted_iota(I32, (2 * SB, 1), 0) < SB
    for c in range(KV_WIDTH // LANES):
        kvar = _head_variants(ck_ref[:, c * LANES:(c + 1) * LANES])
        vvar = _head_variants(cv_ref[:, c * LANES:(c + 1) * LANES])
        kn = kn_ref[:, c * LANES:(c + 1) * LANES].astype(BF16).astype(F32)
        vn = vn_ref[:, c * LANES:(c + 1) * LANES].astype(BF16).astype(F32)
        kn_sw, vn_sw = pltpu.roll(kn, HEAD_DIM, 1), pltpu.roll(vn, HEAD_DIM, 1)
        for e in range(2):
            g = 2 * c + e
            k2 = jnp.concatenate([kvar[2 * e], kvar[2 * e + 1]], axis=0)
            v2 = jnp.concatenate([vvar[2 * e], vvar[2 * e + 1]], axis=0)
            kdup = jnp.where(lo, kn, kn_sw) if e == 0 else jnp.where(lo, kn_sw, kn)
            vdup = jnp.where(lo, vn, vn_sw) if e == 0 else jnp.where(lo, vn_sw, vn)
            qa = q_ref[:, (2 * g) * LANES:(2 * g + 1) * LANES]
            qb = q_ref[:, (2 * g + 1) * LANES:(2 * g + 2) * LANES]
            q4 = jnp.concatenate([qa, qb], axis=0)
            s = lax.dot_general(q4, k2, (((1,), (1,)), ((), ())), preferred_element_type=F32)
            s = jnp.where(own, s, NEG_INF)
            prod = q4.astype(F32) * jnp.concatenate([kdup, kdup], axis=0)
            s_new = (jnp.sum(jnp.where(lo, prod, 0.0), axis=1, keepdims=True),
                     jnp.sum(jnp.where(lo, 0.0, prod), axis=1, keepdims=True))
            ps, pn = [], []
            for hh in range(2):
                sink = jnp.where(first_pair, sink_ref[4 * g + hh], sink_ref[4 * g + 2 + hh])
                p_c, p_n = _softmax_sink(s[:, hh * nkeys:(hh + 1) * nkeys], s_new[hh], sink)
                ps.append(p_c)
                pn.append(p_n)
            p2 = jnp.concatenate(ps, axis=1).astype(BF16)
            o = jnp.dot(p2, v2, preferred_element_type=F32)
            o = o + jnp.where(lo, pn[0], pn[1]) * jnp.concatenate([vdup, vdup], axis=0)
            o_ref[:, (2 * g) * LANES:(2 * g + 1) * LANES] = o[:SB]
            o_ref[:, (2 * g + 1) * LANES:(2 * g + 2) * LANES] = o[SB:]


def _sample_attn(sinks, q, ck2d, cv2d, k_new, v_new):
    b = q.shape[0]
    return pl.pallas_call(
        _sample_attn_kernel,
        out_shape=jax.ShapeDtypeStruct((b, D_MODEL), F32),
        grid=(b // SB,),
        in_specs=[pl.BlockSpec(memory_space=pltpu.SMEM),
                  pl.BlockSpec((SB, D_MODEL), lambda i: (i, 0)),
                  pl.BlockSpec((SB * WINDOW, KV_WIDTH), lambda i: (i, 0)),
                  pl.BlockSpec((SB * WINDOW, KV_WIDTH), lambda i: (i, 0)),
                  pl.BlockSpec((SB, KV_WIDTH), lambda i: (i, 0)),
                  pl.BlockSpec((SB, KV_WIDTH), lambda i: (i, 0))],
        out_specs=pl.BlockSpec((SB, D_MODEL), lambda i: (i, 0)),
        compiler_params=_cparams(),
        name="sample_attn",
    )(sinks, q, ck2d, cv2d, k_new, v_new)


def _sample_merge_kernel(x_ref, mods_ref, attn_ref, pool_ref, ga_ref, gb_ref, wout_ref, g1_ref, b1_ref,
                         x1_ref, h2_ref):
    merged = jax.nn.sigmoid(ga_ref[...]) * attn_ref[...] + jax.nn.sigmoid(gb_ref[...]) * pool_ref[...]
    y = jnp.dot(merged.astype(BF16), wout_ref[...], preferred_element_type=F32)
    x1 = _layer_norm(ALPHA * x_ref[...] + mods_ref[:, 2048:3072] * y, g1_ref[...], b1_ref[...])
    x1_ref[...] = x1
    h2_ref[...] = _pack_bf16_pairs(x1 * (1.0 + mods_ref[:, 4096:5120]) + mods_ref[:, 3072:4096])


def _sample_merge(x_s, mods_s, attn, pool, ga, gb, w_out_b, ln1_g, ln1_b):
    b = x_s.shape[0]
    return pl.pallas_call(
        _sample_merge_kernel,
        out_shape=(jax.ShapeDtypeStruct((b, D_MODEL), F32), jax.ShapeDtypeStruct((b, D_MODEL // 2), U32)),
        compiler_params=pltpu.CompilerParams(vmem_limit_bytes=VMEM_LIMIT_BYTES),
        name="sample_merge",
    )(x_s, mods_s, attn, pool, ga, gb, w_out_b, ln1_g, ln1_b)


def _router_kernel(h2_ref, wrt_ref, bias_ref, idx_ref, wts_ref, rank_ref, cnt_ref, carry_s):
    i = pl.program_id(0)

    @pl.when(i == 0)
    def _():
        carry_s[...] = jnp.zeros_like(carry_s)

    lo, hi = _unpack_bf16_pairs(h2_ref[...])
    nt = (((1,), (1,)), ((), ()))
    half = D_MODEL // 2
    logits = (lax.dot_general(wrt_ref[:, :half], lo.astype(BF16), nt, preferred_element_type=F32)
              + lax.dot_general(wrt_ref[:, half:], hi.astype(BF16), nt, preferred_element_type=F32))
    scores = jax.nn.sigmoid(logits)
    sel = scores + bias_ref[...]

    gi = lax.broadcasted_iota(I32, (EXPERTS_PER_GROUP, TB), 0).astype(F32)
    gscore = []
    for g in range(N_EXPERT_GROUPS):
        blk = sel[g * EXPERTS_PER_GROUP:(g + 1) * EXPERTS_PER_GROUP, :]
        m1 = jnp.max(blk, axis=0, keepdims=True)
        f1 = jnp.min(jnp.where(blk == m1, gi, float(EXPERTS_PER_GROUP)), axis=0, keepdims=True)
        m2 = jnp.max(jnp.where(gi == f1, NEG_INF, blk), axis=0, keepdims=True)
        gscore.append(m1 + m2)

    gsel = [jnp.zeros((1, TB), jnp.bool_) for _ in range(N_EXPERT_GROUPS)]
    for _ in range(TOPK_GROUPS):
        m = gscore[0]
        for g in range(1, N_EXPERT_GROUPS):
            m = jnp.maximum(m, gscore[g])
        found = jnp.zeros((1, TB), jnp.bool_)
        for g in range(N_EXPERT_GROUPS):
            pick = jnp.logical_and(gscore[g] == m, jnp.logical_not(found))
            found = jnp.logical_or(found, pick)
            gsel[g] = jnp.logical_or(gsel[g], pick)
            gscore[g] = jnp.where(pick, NEG_INF, gscore[g])
    cur = jnp.concatenate(
        [jnp.where(gsel[g], sel[g * EXPERTS_PER_GROUP:(g + 1) * EXPERTS_PER_GROUP, :], NEG_INF)
         for g in range(N_EXPERT_GROUPS)], axis=0)

    ei = lax.broadcasted_iota(I32, (N_EXPERTS, TB), 0).astype(F32)
    onehot = jnp.zeros((N_EXPERTS, TB), F32)
    picks, raw_w = [], []
    for _ in range(TOP_K):
        m = jnp.max(cur, axis=0, keepdims=True)
        f = jnp.min(jnp.where(cur == m, ei, float(N_EXPERTS)), axis=0, keepdims=True)
        pick = ei == f
        picks.append(f)
        raw_w.append(jnp.sum(jnp.where(pick, scores, 0.0), axis=0, keepdims=True))
        cur = jnp.where(pick, NEG_INF, cur)
        onehot = onehot + pick.astype(F32)
    wsum = raw_w[0]
    for k in range(1, TOP_K):
        wsum = wsum + raw_w[k]

    tri = (lax.broadcasted_iota(I32, (TB, TB), 0) <= lax.broadcasted_iota(I32, (TB, TB), 1)).astype(BF16)
    incl = jnp.dot(onehot.astype(BF16), tri, preferred_element_type=F32)
    pos = carry_s[...] + incl - onehot
    for k in range(TOP_K):
        idx_ref[k:k + 1, :] = picks[k].astype(I32)
        wts_ref[k:k + 1, :] = raw_w[k] / wsum * ROUTED_SCALE
        rank_ref[k:k + 1, :] = jnp.sum(jnp.where(ei == picks[k], pos, 0.0), axis=0, keepdims=True).astype(I32)
    carry = carry_s[...] + jnp.sum(onehot, axis=1, keepdims=True)
    carry_s[...] = carry
    cnt_ref[...] = carry


def _router(h2_all, w_router_t_b, router_bias):
    t_all = h2_all.shape[0]
    tok = lambda dt: jax.ShapeDtypeStruct((TOP_K, t_all), dt)
    return pl.pallas_call(
        _router_kernel,
        out_shape=(tok(I32), tok(F32), tok(I32), jax.ShapeDtypeStruct((N_EXPERTS, LANES), F32)),
        grid=(t_all // TB,),
        in_specs=[pl.BlockSpec((TB, D_MODEL // 2), lambda i: (i, 0)),
                  pl.BlockSpec((N_EXPERTS, D_MODEL), lambda i: (0, 0)),
                  pl.BlockSpec((N_EXPERTS, 1), lambda i: (0, 0))],
        out_specs=(pl.BlockSpec((TOP_K, TB), lambda i: (0, i)),
                   pl.BlockSpec((TOP_K, TB), lambda i: (0, i)),
                   pl.BlockSpec((TOP_K, TB), lambda i: (0, i)),
                   pl.BlockSpec((N_EXPERTS, LANES), lambda i: (0, 0))),
        scratch_shapes=[pltpu.VMEM((N_EXPERTS, LANES), F32)],
        compiler_params=_cparams(),
        name="router",
    )(h2_all, w_router_t_b, router_bias.reshape(N_EXPERTS, 1))


def _moe_meta_kernel(idx_ref, rank_ref, cnt_ref, dest_ref, emeta_ref, tstart_s):
    i = pl.program_id(0)

    @pl.when(i == 0)
    def _():
        cnt_col = cnt_ref[:, 0:1]
        tiles_col = jnp.floor((cnt_col + float(ROW_TILE - 1)) * (1.0 / ROW_TILE))
        e_r = lax.broadcasted_iota(I32, (N_EXPERTS, N_EXPERTS), 0)
        e_c = lax.broadcasted_iota(I32, (N_EXPERTS, N_EXPERTS), 1)
        diag = e_r == e_c
        tiles_row = jnp.sum(jnp.where(diag, tiles_col, 0.0), axis=0, keepdims=True)
        cnt_row = jnp.sum(jnp.where(diag, cnt_col, 0.0), axis=0, keepdims=True)
        tstart_row = jnp.sum(jnp.where(e_r < e_c, tiles_col, 0.0), axis=0, keepdims=True)
        tstart_col = jnp.sum(jnp.where(e_c < e_r, tiles_row, 0.0), axis=1, keepdims=True)
        tstart_s[...] = jnp.broadcast_to(tstart_col, (N_EXPERTS, LANES))
        emeta_ref[0:1, :] = tstart_row.astype(I32)
        emeta_ref[1:2, :] = tiles_row.astype(I32)
        emeta_ref[2:3, :] = cnt_row.astype(I32)
        later_used = jnp.logical_and(e_r > e_c, tiles_col > 0.0)
        next_used = jnp.min(jnp.where(later_used, e_r.astype(F32), float(N_EXPERTS)), axis=0, keepdims=True)
        emeta_ref[3:4, :] = next_used.astype(I32)
        emeta_ref[4:8, :] = jnp.zeros((4, N_EXPERTS), I32)

    ei = lax.broadcasted_iota(I32, (N_EXPERTS, TB), 0)
    tstart = tstart_s[...]
    for k in range(TOP_K):
        st = jnp.sum(jnp.where(ei == idx_ref[k:k + 1, :], tstart, 0.0), axis=0, keepdims=True)
        dest_ref[k:k + 1, :] = st.astype(I32) * ROW_TILE + rank_ref[k:k + 1, :]


def _moe_meta(idx, rank, counts):
    t_all = idx.shape[1]
    return pl.pallas_call(
        _moe_meta_kernel,
        out_shape=(jax.ShapeDtypeStruct((TOP_K, t_all), I32), jax.ShapeDtypeStruct((8, N_EXPERTS), I32)),
        grid=(t_all // TB,),
        in_specs=[pl.BlockSpec((TOP_K, TB), lambda i: (0, i)),
                  pl.BlockSpec((TOP_K, TB), lambda i: (0, i)),
                  pl.BlockSpec((N_EXPERTS, LANES), lambda i: (0, 0))],
        out_specs=(pl.BlockSpec((TOP_K, TB), lambda i: (0, i)),
                   pl.BlockSpec((8, N_EXPERTS), lambda i: (0, 0))),
        scratch_shapes=[pltpu.VMEM((N_EXPERTS, LANES), F32)],
        compiler_params=_cparams(),
        name="moe_meta",
    )(idx, rank, counts)


def _zero_tail(zbuf, out_ref, zsem, base):
    zbuf[...] = jnp.zeros_like(zbuf)
    copies = [pltpu.make_async_copy(zbuf, out_ref.at[pl.ds(base + j * ZROWS, ZROWS), :], zsem)
              for j in range(TAIL_ROWS // ZROWS)]
    for c in copies:
        c.start()
    for c in copies:
        c.wait()


def _dispatch_kernel(emeta_ref, dest_ref, h2_ref, xs_ref, zbuf, sem, zsem, *, base):
    i = pl.program_id(0)

    @pl.when(i == 0)
    def _():
        _zero_tail(zbuf, xs_ref, zsem, base)

        def gap_rows(e, fn):
            cnt = emeta_ref[2 * N_EXPERTS + e]
            first = emeta_ref[e] * ROW_TILE + cnt
            n_gap = emeta_ref[N_EXPERTS + e] * ROW_TILE - cnt

            def body(r, carry):
                fn(pltpu.make_async_copy(zbuf.at[pl.ds(0, 1), :], xs_ref.at[pl.ds(first + r, 1), :], zsem))
                return carry

            lax.fori_loop(0, n_gap, body, 0)

        def start_gaps(e, carry):
            gap_rows(e, lambda c: c.start())
            return carry

        def wait_gaps(e, carry):
            gap_rows(e, lambda c: c.wait())
            return carry

        lax.fori_loop(0, N_EXPERTS, start_gaps, 0)
        lax.fori_loop(0, N_EXPERTS, wait_gaps, 0)

    def row_copy(a, b, k):
        d = dest_ref[0, 0, k * TBD + a * ROW_TILE + b]
        return pltpu.make_async_copy(h2_ref.at[a, pl.ds(b, 1), :], xs_ref.at[pl.ds(d, 1), :], sem)

    def for_rows(fn):
        def body(a, carry):
            for b in range(ROW_TILE):
                for k in range(TOP_K):
                    fn(row_copy(a, b, k), k)
            return carry

        lax.fori_loop(0, TBD // ROW_TILE, body, 0)

    for_rows(lambda c, k: c.start(priority=k % 2))
    for_rows(lambda c, k: c.wait())


def _dispatch(emeta_flat, dest_blk, h2_all):
    t_all = h2_all.shape[0]
    base = t_all * TOP_K
    grid_spec = pltpu.PrefetchScalarGridSpec(
        num_scalar_prefetch=1,
        grid=(t_all // TBD,),
        in_specs=[pl.BlockSpec((1, 1, TOP_K * TBD), lambda i, m: (i, 0, 0), memory_space=pltpu.SMEM),
                  pl.BlockSpec((TBD // ROW_TILE, ROW_TILE, D_MODEL // 2), lambda i, m: (i, 0, 0))],
        out_specs=pl.BlockSpec(memory_space=pl.ANY),
        scratch_shapes=[pltpu.VMEM((ZROWS, D_MODEL // 2), U32),
                        pltpu.SemaphoreType.DMA(()), pltpu.SemaphoreType.DMA(())],
    )
    return pl.pallas_call(
        functools.partial(_dispatch_kernel, base=base),
        out_shape=jax.ShapeDtypeStruct((base + TAIL_ROWS, D_MODEL // 2), U32),
        grid_spec=grid_spec,
        compiler_params=_cparams(),
        name="dispatch",
    )(emeta_flat, dest_blk, h2_all.reshape(t_all // ROW_TILE, ROW_TILE, D_MODEL // 2))


def _experts_kernel(emeta_ref, xs_ref, wg_ref, wu_ref, wd_ref, ys_ref,
                    xbuf, ybuf, zbuf, wgu_s, wd_s, st, pend_row, pend_n, isem, osem, zsem, *, base):
    step = pl.program_id(0)

    def load(row, slot):
        return pltpu.make_async_copy(xs_ref.at[pl.ds(pl.multiple_of(row, ROW_TILE), CH), :],
                                     xbuf.at[slot], isem.at[slot])

    def request_next(slot):
        ce, cj = st[1], st[2]

        @pl.when(ce < N_EXPERTS)
        def _():
            load(emeta_ref[ce] * ROW_TILE + cj * CH, slot).start()
            c_chunks = (emeta_ref[N_EXPERTS + ce] + (CH_TILES - 1)) >> CH_SHIFT
            more = cj + 1 < c_chunks
            st[1] = jnp.where(more, ce, emeta_ref[3 * N_EXPERTS + ce])
            st[2] = jnp.where(more, cj + 1, 0)

    def out_copies(slot, fn):
        row, n = pend_row[slot], pend_n[slot]

        @pl.when(n == CH_TILES + 1)
        def _():
            fn(pltpu.make_async_copy(ybuf.at[slot], ys_ref.at[pl.ds(pl.multiple_of(row, ROW_TILE), CH), :],
                                     osem.at[slot]))

        def body(t, carry):
            fn(pltpu.make_async_copy(
                ybuf.at[slot, pl.ds(pl.multiple_of(t * ROW_TILE, ROW_TILE), ROW_TILE), :],
                ys_ref.at[pl.ds(pl.multiple_of(row + t * ROW_TILE, ROW_TILE), ROW_TILE), :], osem.at[slot]))
            return carry

        lax.fori_loop(0, jnp.where(n <= CH_TILES, n, 0), body, 0)

    def run_expert(e, sub):
        _expert_body(e, sub, emeta_ref, wg_ref, wu_ref, wd_ref, xbuf, ybuf, wgu_s, wd_s, st, pend_row, pend_n,
                     load, request_next, out_copies)

    @pl.when(step == 0)
    def _():
        _zero_tail(zbuf, ys_ref, zsem, base)
        st[0] = 0
        st[1] = jnp.where(emeta_ref[N_EXPERTS] > 0, 0, emeta_ref[3 * N_EXPERTS])
        st[2] = 0
        pend_n[0] = 0
        pend_n[1] = 0
        for s in range(N_XBUF - 1):
            request_next(s)

    for sub in range(EPS):
        run_expert(step * EPS + sub, sub)

    @pl.when(step == pl.num_programs(0) - 1)
    def _():
        for s in range(2):
            out_copies(s, lambda c: c.wait())
            pend_n[s] = 0


def _expert_body(e, sub, emeta_ref, wg_ref, wu_ref, wd_ref, xbuf, ybuf, wgu_s, wd_s, st, pend_row, pend_n,
                 load, request_next, out_copies):
    row0 = emeta_ref[e] * ROW_TILE
    n_tiles = emeta_ref[N_EXPERTS + e]
    n_chunks = (n_tiles + (CH_TILES - 1)) >> CH_SHIFT

    @pl.when(n_tiles > 0)
    def _():
        wgu_s[:, :EXPERT_DIM] = wg_ref[sub].astype(BF16)
        wgu_s[:, EXPERT_DIM:] = wu_ref[sub].astype(BF16)
        wd_s[...] = wd_ref[sub].astype(BF16)
        g0 = st[0]

        def chunk(j, carry):
            g = g0 + j
            slot = g & (N_XBUF - 1)
            load(row0 + j * CH, slot).wait()
            request_next((g + N_XBUF - 1) & (N_XBUF - 1))

            lo, hi = _unpack_bf16_pairs(xbuf[slot])
            half = D_MODEL // 2
            gu = (jnp.dot(lo.astype(BF16), wgu_s[:half, :], preferred_element_type=F32)
                  + jnp.dot(hi.astype(BF16), wgu_s[half:, :], preferred_element_type=F32))
            act = _silu(gu[:, :EXPERT_DIM]) * gu[:, EXPERT_DIM:]
            y = _pack_bf16_pairs(jnp.dot(act.astype(BF16), wd_s[...], preferred_element_type=F32))

            oslot = g & 1
            out_copies(oslot, lambda c: c.wait())
            ybuf[oslot] = y
            pend_row[oslot] = row0 + j * CH
            pend_n[oslot] = jnp.where((j + 1) * CH_TILES <= n_tiles, CH_TILES + 1, n_tiles - j * CH_TILES)
            out_copies(oslot, lambda c: c.start())
            return carry

        lax.fori_loop(0, n_chunks, chunk, 0)
        st[0] = g0 + n_chunks


def _experts(emeta_flat, xs, w_gate, w_up, w_down):
    base = xs.shape[0] - TAIL_ROWS
    w_map = lambda e, m: (e, 0, 0)
    grid_spec = pltpu.PrefetchScalarGridSpec(
        num_scalar_prefetch=1,
        grid=(N_EXPERTS // EPS,),
        in_specs=[pl.BlockSpec(memory_space=pl.ANY),
                  pl.BlockSpec((EPS, D_MODEL, EXPERT_DIM), w_map),
                  pl.BlockSpec((EPS, D_MODEL, EXPERT_DIM), w_map),
                  pl.BlockSpec((EPS, EXPERT_DIM, D_MODEL), w_map)],
        out_specs=pl.BlockSpec(memory_space=pl.ANY),
        scratch_shapes=[pltpu.VMEM((N_XBUF, CH, D_MODEL // 2), U32),
                        pltpu.VMEM((2, CH, D_MODEL // 2), U32),
                        pltpu.VMEM((ZROWS, D_MODEL // 2), U32),
                        pltpu.VMEM((D_MODEL, 2 * EXPERT_DIM), BF16),
                        pltpu.VMEM((EXPERT_DIM, D_MODEL), BF16),
                        pltpu.SMEM((3,), I32), pltpu.SMEM((2,), I32), pltpu.SMEM((2,), I32),
                        pltpu.SemaphoreType.DMA((N_XBUF,)), pltpu.SemaphoreType.DMA((2,)),
                        pltpu.SemaphoreType.DMA(())],
    )
    return pl.pallas_call(
        functools.partial(_experts_kernel, base=base),
        out_shape=jax.ShapeDtypeStruct(xs.shape, U32),
        grid_spec=grid_spec,
        compiler_params=_cparams(),
        name="experts",
    )(emeta_flat, xs, w_gate, w_up, w_down)


def _combine_kernel(dest_ref, dnext_ref, x1_ref, mods_ref, wts_ref, ys_ref, wsgu_ref, wsd_ref, g2_ref, b2_ref,
                    out_ref, buf, shared_s, sem):
    i = pl.program_id(0)
    slot = i & 1

    def row_copy(dref, s, a, b, k):
        return pltpu.make_async_copy(ys_ref.at[pl.ds(dref[0, 0, k * TB + a * ROW_TILE + b], 1), :],
                                     buf.at[s, k, a, pl.ds(b, 1), :], sem.at[s])

    def for_rows(fn):
        def body(a, carry):
            for b in range(ROW_TILE):
                for k in range(TOP_K):
                    fn(a, b, k)
            return carry

        lax.fori_loop(0, TB // ROW_TILE, body, 0)

    def gather(dref, s):
        for a in range(TB // ROW_TILE):
            for b in range(ROW_TILE):
                for k in range(TOP_K):
                    row_copy(dref, s, a, b, k).start(priority=k % 2)

    @pl.when(i == 0)
    def _():
        gather(dest_ref, 0)

    def shared_expert():
        h2 = (x1_ref[...] * (1.0 + mods_ref[:, 4096:5120]) + mods_ref[:, 3072:4096]).astype(BF16)
        gu = jnp.dot(h2, wsgu_ref[...], preferred_element_type=F32)
        act = _silu(gu[:, :SHARED_DIM]) * gu[:, SHARED_DIM:]
        shared_s[...] = jnp.dot(act.astype(BF16), wsd_ref[...], preferred_element_type=F32)

    @pl.when(i + 1 < pl.num_programs(0))
    def _():
        for s in range(2):
            @pl.when(slot == 1 - s)
            def _(s=s):
                shared_expert()
                gather(dnext_ref, s)

    @pl.when(i + 1 >= pl.num_programs(0))
    def _():
        shared_expert()

    x1 = x1_ref[...]
    shared = shared_s[...]

    for_rows(lambda a, b, k: row_copy(dest_ref, slot, a, b, k).wait())

    acc_lo = jnp.zeros((TB, D_MODEL // 2), F32)
    acc_hi = jnp.zeros((TB, D_MODEL // 2), F32)
    for k in range(TOP_K):
        lo, hi = _unpack_bf16_pairs(buf[slot, k].reshape(TB, D_MODEL // 2))
        wk = wts_ref[:, k:k + 1]
        acc_lo = acc_lo + wk * lo
        acc_hi = acc_hi + wk * hi
    m = shared + jnp.concatenate([acc_lo, acc_hi], axis=1)
    out_ref[...] = _layer_norm(ALPHA * x1 + mods_ref[:, 5120:6144] * m, g2_ref[...], b2_ref[...])


def _combine(dest_blk, x1, mods, wts_t, ys, wsgu_b, wsd_b, ln2_g, ln2_b, blk_off):
    t = x1.shape[0]
    per_row = mods.shape[0] != 1
    mods_spec = (pl.BlockSpec((TB, 6 * D_MODEL), lambda i: (i, 0)) if per_row
                 else pl.BlockSpec((1, 6 * D_MODEL), lambda i: (0, 0)))
    full = lambda shape: pl.BlockSpec(shape, lambda i: (0,) * len(shape))
    n_steps = t // TB
    return pl.pallas_call(
        _combine_kernel,
        out_shape=jax.ShapeDtypeStruct((t, D_MODEL), F32),
        grid=(n_steps,),
        in_specs=[pl.BlockSpec((1, 1, TOP_K * TB), lambda i: (i + blk_off, 0, 0), memory_space=pltpu.SMEM),
                  pl.BlockSpec((1, 1, TOP_K * TB), lambda i: (jnp.minimum(i + 1, n_steps - 1) + blk_off, 0, 0),
                               memory_space=pltpu.SMEM),
                  pl.BlockSpec((TB, D_MODEL), lambda i: (i, 0)),
                  mods_spec,
                  pl.BlockSpec((TB, TOP_K), lambda i: (i + blk_off, 0)),
                  pl.BlockSpec(memory_space=pl.ANY),
                  full((D_MODEL, 2 * SHARED_DIM)), full((SHARED_DIM, D_MODEL)),
                  full((1, D_MODEL)), full((1, D_MODEL))],
        out_specs=pl.BlockSpec((TB, D_MODEL), lambda i: (i, 0)),
        scratch_shapes=[pltpu.VMEM((2, TOP_K, TB // ROW_TILE, ROW_TILE, D_MODEL // 2), U32),
                        pltpu.VMEM((TB, D_MODEL), F32),
                        pltpu.SemaphoreType.DMA((2,))],
        compiler_params=_cparams(),
        name="combine",
    )(dest_blk, dest_blk, x1, mods, wts_t, ys, wsgu_b, wsd_b, ln2_g, ln2_b)


def _rope_tables(pos):
    half = ROT_DIM // 2
    inv = ROPE_THETA ** (-jnp.arange(0, ROT_DIM, 2, dtype=F32) / ROT_DIM)
    ang = pos.astype(F32)[:, None] * inv[None, :]
    cos, sin = lax.optimization_barrier((jnp.cos(ang), jnp.sin(ang)))
    n = pos.shape[0]
    ones = jnp.ones((n, HEAD_DIM - ROT_DIM), F32)
    z = lambda w: jnp.zeros((n, w), F32)
    ra = jnp.concatenate([cos, cos, ones], axis=1)
    rb = jnp.concatenate([-sin, z(HEAD_DIM - half)], axis=1)
    rc = jnp.concatenate([z(half), sin, z(HEAD_DIM - ROT_DIM)], axis=1)
    return tuple(jnp.tile(t, (1, LANES // HEAD_DIM)) for t in (ra, rb, rc))


def kernel(x_prompt, x_sample, cache_k, cache_v, state_pool, c_prompt, c_sample, w_ada, b_ada, w_in, attn_sinks, w_pool, pool_scale, w_out, ln1_g, ln1_b, w_router, router_bias, w_exp_gate, w_exp_up, w_exp_down, w_sh_gate, w_sh_up, w_sh_down, ln2_g, ln2_b):
    assert w_ada.shape[0] == DEPTH == 1 and x_prompt.shape[0] == 1 and x_sample.shape[1] == 1
    t_p, n_s = x_prompt.shape[1], x_sample.shape[0]
    assert t_p % TQ == 0 and n_s % SB == 0 and n_s == TB and cache_k.shape[2] == WINDOW
    t_all = t_p + n_s
    assert t_all % TBD == 0

    row = lambda a: a[0].reshape(1, -1)
    w_in_b, w_out_b, w_pool_b = w_in[0].astype(BF16), w_out[0].astype(BF16), w_pool[0].astype(BF16)
    wsgu_b = jnp.concatenate([w_sh_gate[0], w_sh_up[0]], axis=1).astype(BF16)
    wsd_b = w_sh_down[0].astype(BF16)
    w_router_t_b = w_router[0].T.astype(BF16)
    sinks = attn_sinks[0]
    pscale, g1, b1, g2, b2 = row(pool_scale), row(ln1_g), row(ln1_b), row(ln2_g), row(ln2_b)

    x_s = x_sample.reshape(n_s, D_MODEL)
    c_all = jnp.concatenate([c_sample, c_prompt, jnp.zeros((7, D_MODEL), F32)], axis=0)
    mods = _ada(c_all, w_ada[0], b_ada[0])
    mods_s, mods_p = mods[:n_s], mods[n_s:n_s + 1]

    x_p = x_prompt.reshape(t_p, D_MODEL)
    x1_p, h2_p, k_win, v_win, pool_st = _mixer_prompt(
        x_p, mods_p, _rope_tables(jnp.arange(t_p)), w_in_b, sinks, w_pool_b, pscale, w_out_b, g1, b1)

    sp_t = jnp.transpose(state_pool[0], (1, 0, 2))
    q_s, k_new, v_new, u_new, pool_s, ga_s, gb_s = _sample_proj(
        x_s, mods_s, _rope_tables(jnp.full((1,), PAST_LEN)), w_in_b, sp_t, w_pool_b, pscale)
    attn_s = _sample_attn(sinks, q_s, cache_k[0].reshape(n_s * WINDOW, KV_WIDTH),
                          cache_v[0].reshape(n_s * WINDOW, KV_WIDTH), k_new, v_new)
    x1_s, h2_s = _sample_merge(x_s, mods_s, attn_s, pool_s, ga_s, gb_s, w_out_b, g1, b1)

    h2_all = jnp.concatenate([h2_p, h2_s], axis=0)
    idx, wts, rank, counts = _router(h2_all, w_router_t_b, router_bias[0])
    dest, emeta = _moe_meta(idx, rank, counts)
    emeta_flat = emeta[:4].reshape(-1)
    n_tb = t_all // TB
    dest_blk = dest.reshape(TOP_K, n_tb, TB).transpose(1, 0, 2).reshape(n_tb, 1, TOP_K * TB)
    n_tbd = t_all // TBD
    dest_blk_d = dest.reshape(TOP_K, n_tbd, TBD).transpose(1, 0, 2).reshape(n_tbd, 1, TOP_K * TBD)
    xs = _dispatch(emeta_flat, dest_blk_d, h2_all)
    ys = _experts(emeta_flat, xs, w_exp_gate[0], w_exp_up[0], w_exp_down[0])
    wts_t = wts.T
    y_p = _combine(dest_blk, x1_p, mods_p, wts_t, ys, wsgu_b, wsd_b, g2, b2, 0)
    y_s = _combine(dest_blk, x1_s, mods_s, wts_t, ys, wsgu_b, wsd_b, g2, b2, t_p // TB)

    k_win_s = jnp.concatenate([cache_k[0][:, 1:], k_new.reshape(n_s, 1, N_KV_HEADS, HEAD_DIM)], axis=1)
    v_win_s = jnp.concatenate([cache_v[0][:, 1:], v_new.reshape(n_s, 1, N_KV_HEADS, HEAD_DIM)], axis=1)
    pool_s_out = jnp.concatenate([state_pool[0][:, 1:], u_new[:, None, :]], axis=1)
    return (y_p.reshape(1, t_p, D_MODEL), y_s.reshape(n_s, 1, D_MODEL),
            k_win.reshape(1, 1, WINDOW, N_KV_HEADS, HEAD_DIM), v_win.reshape(1, 1, WINDOW, N_KV_HEADS, HEAD_DIM),
            pool_st.reshape(1, 1, POOL_HALO, D_MODEL),
            k_win_s[None], v_win_s[None], pool_s_out[None])
```
